```python
import jax
import jax.numpy as jnp
from jax import lax
import numpy as np

D_MODEL = 1024
BATCH = 32
SEQ = 2048
DEPTH = 4

CHUNK = 64
RET_HEADS = 4
RET_HD = 64
RET_W = RET_HEADS * RET_HD
RWKV_HEADS = 6
RWKV_HD = 64
RWKV_W = RWKV_HEADS * RWKV_HD
MLSTM_HEADS = 4
MLSTM_HD = 96
MLSTM_W = MLSTM_HEADS * MLSTM_HD
D_MIX = RET_W + RWKV_W + MLSTM_W
RWKV_DECAY_LORA = 64
RWKV_ICL_LORA = 64
RWKV_GATE_LORA = 128
MLSTM_CONV = 4
D_FF = ((8 * D_MODEL + 3 * 256 - 1) // (3 * 256)) * 256
ROPE_BASE = 10000.0
RMS_EPS = 1e-6
GN_EPS = 1e-5

RET_SIZES = (RET_W, RET_W, RET_W, RET_W)
RWKV_SIZES = (RWKV_W, RWKV_W, RWKV_W, RWKV_DECAY_LORA, RWKV_ICL_LORA, RWKV_GATE_LORA)
MLSTM_SIZES = (2 * MLSTM_W, MLSTM_W, MLSTM_W, MLSTM_HEADS, MLSTM_HEADS)
RET_IN = sum(RET_SIZES)
RWKV_IN = sum(RWKV_SIZES)
MLSTM_IN = sum(MLSTM_SIZES)
D_IN = RET_IN + RWKV_IN + MLSTM_IN

kernel_name = 'hybrid_ret_rwkv7_mlstm_adaln_trunk'

F32 = jnp.float32


def _split(x, sizes):
    return jnp.split(x, [int(s) for s in np.cumsum(sizes)[:-1]], axis=-1)


def rms_norm(x, gain):
    xf = x.astype(F32)
    y = xf * lax.rsqrt(jnp.mean(xf * xf, axis=-1, keepdims=True) + RMS_EPS)
    return (y * gain.astype(F32)).astype(x.dtype)


def head_norm(x, gain, n_heads):
    b, t, w = x.shape
    xh = x.reshape(b, t, n_heads, w // n_heads)
    mu = jnp.mean(xh, axis=-1, keepdims=True)
    xc = xh - mu
    var = jnp.mean(xc * xc, axis=-1, keepdims=True)
    return (xc * lax.rsqrt(var + GN_EPS)).reshape(b, t, w) * gain.astype(F32)


def rotary(x):
    t, d = x.shape[1], x.shape[-1]
    inv = ROPE_BASE ** (-jnp.arange(0, d, 2, dtype=F32) / d)
    ang = jnp.arange(t, dtype=F32)[:, None] * inv[None, :]
    cos = jnp.cos(ang)[None, :, None, :]
    sin = jnp.sin(ang)[None, :, None, :]
    x1, x2 = jnp.split(x, 2, axis=-1)
    return jnp.concatenate([x1 * cos - x2 * sin, x1 * sin + x2 * cos], axis=-1)


def to_chunks(x):
    b, t, h = x.shape[:3]
    x = x.reshape((b, t // CHUNK, CHUNK, h) + x.shape[3:])
    return jnp.transpose(x, (1, 0, 3, 2) + tuple(range(4, x.ndim)))


def from_chunks(y):
    nc, b, h, l, d = y.shape
    return jnp.transpose(y, (1, 0, 3, 2, 4)).reshape(b, nc * l, h * d)


def token_shift(x, mu):
    xprev = jnp.pad(x, ((0, 0), (1, 0), (0, 0)))[:, :-1]
    return x + (xprev - x) * mu


def causal_conv(x, w, b):
    t = x.shape[1]
    xp = jnp.pad(x, ((0, 0), (MLSTM_CONV - 1, 0), (0, 0)))
    y = b + xp[:, 0:t] * w[0]
    for j in range(1, MLSTM_CONV):
        y = y + xp[:, j:j + t] * w[j]
    return y


def retention(q, k, v):
    b, t, h, d = q.shape
    log_g = jnp.log1p(-jnp.exp2(-5.0 - jnp.arange(h, dtype=F32)))
    idx = jnp.arange(CHUNK, dtype=F32)
    diff = idx[:, None] - idx[None, :]
    intra_decay = jnp.where(diff >= 0, jnp.exp(log_g[:, None, None] * jnp.maximum(diff, 0.0)), 0.0)
    q_decay = jnp.exp(log_g[:, None] * (idx + 1.0))[None, :, :, None]
    k_decay = jnp.exp(log_g[:, None] * (CHUNK - 1.0 - idx))[None, :, :, None]
    chunk_decay = jnp.exp(log_g * CHUNK)[None, :, None, None]
    k = k * (d ** -0.5)

    def step(state, inp):
        qc, kc, vc = inp
        s = jnp.einsum('bhld,bhmd->bhlm', qc, kc) * intra_decay
        y = jnp.einsum('bhlm,bhme->bhle', s, vc) + jnp.einsum('bhld,bhde->bhle', qc, state) * q_decay
        state = state * chunk_decay + jnp.einsum('bhld,bhle->bhde', kc * k_decay, vc)
        return state, y

    init = jnp.zeros((b, h, d, d), F32)
    _, y = lax.scan(step, init, (to_chunks(q), to_chunks(k), to_chunks(v)))
    return from_chunks(y)


def rwkv7_scan(r, decay, k, v, a, kk):
    b, t, h, n = r.shape

    def step(S, inp):
        r_t, w_t, k_t, v_t, a_t, kk_t = inp
        sa = jnp.einsum('bhij,bhj->bhi', S, -kk_t)
        S = S * w_t[:, :, None, :] + sa[..., None] * (kk_t * a_t)[:, :, None, :] + v_t[..., None] * k_t[:, :, None, :]
        return S, jnp.einsum('bhij,bhj->bhi', S, r_t)

    xs = tuple(jnp.moveaxis(z, 1, 0) for z in (r, decay, k, v, a, kk))
    _, y = lax.scan(step, jnp.zeros((b, h, n, n), F32), xs)
    return jnp.moveaxis(y, 0, 1)


def mlstm_chunkwise(q, k, v, i_pre, log_f):
    b, t, h, d = q.shape
    k = k * (d ** -0.5)
    causal = jnp.tril(jnp.ones((CHUNK, CHUNK), dtype=bool))

    def step(carry, inp):
        C, n, m = carry
        qc, kc, vc, ic, fc = inp
        bcum = jnp.cumsum(fc, axis=-1)
        a_inter = bcum + m[..., None]
        D = bcum[..., :, None] - bcum[..., None, :] + ic[..., None, :]
        D = jnp.where(causal, D, -jnp.inf)
        m_t = jnp.maximum(a_inter, jnp.max(D, axis=-1))
        inter_w = jnp.exp(a_inter - m_t)
        s = jnp.einsum('bhld,bhmd->bhlm', qc, kc) * jnp.exp(D - m_t[..., None])
        num = jnp.einsum('bhlm,bhme->bhle', s, vc) + inter_w[..., None] * jnp.einsum('bhld,bhde->bhle', qc, C)
        den = jnp.sum(s, axis=-1) + inter_w * jnp.einsum('bhld,bhd->bhl', qc, n)
        hc = num / jnp.maximum(jnp.abs(den), jnp.exp(-m_t))[..., None]
        b_last = bcum[..., -1]
        g_s = b_last[..., None] - bcum + ic
        m_new = jnp.maximum(b_last + m, jnp.max(g_s, axis=-1))
        ws = jnp.exp(g_s - m_new[..., None])[..., None]
        carry_decay = jnp.exp(b_last + m - m_new)
        C = carry_decay[..., None, None] * C + jnp.einsum('bhld,bhle->bhde', kc * ws, vc)
        n = carry_decay[..., None] * n + jnp.sum(kc * ws, axis=2)
        return (C, n, m_new), hc

    init = (jnp.zeros((b, h, d, d), F32), jnp.zeros((b, h, d), F32), jnp.zeros((b, h), F32))
    _, y = lax.scan(step, init, (to_chunks(q), to_chunks(k), to_chunks(v), to_chunks(i_pre), to_chunks(log_f)))
    return from_chunks(y)


def retention_group(cols, gain):
    b, t, _ = cols.shape
    q, k, v, g = _split(cols.astype(F32), RET_SIZES)
    heads = lambda z: z.reshape(b, t, RET_HEADS, RET_HD)
    y = retention(rotary(heads(q)), rotary(heads(k)), heads(v))
    return head_norm(y, gain, RET_HEADS) * jax.nn.silu(g)


def rwkv_group(cols, gain, mu, w0, w2, a0, a2, g2, k_k, k_a, r_k):
    b, t, _ = cols.shape
    cols = token_shift(cols.astype(F32), mu.astype(F32))
    r, k, v, wl, al, gl = _split(cols, RWKV_SIZES)
    log_w = jax.nn.log_sigmoid(w0.astype(F32) + jnp.tanh(wl) @ w2.astype(F32)) - 0.5
    decay = jnp.exp(-jnp.exp(log_w))
    a = jax.nn.sigmoid(a0.astype(F32) + al @ a2.astype(F32))
    g = jax.nn.sigmoid(gl) @ g2.astype(F32)
    heads = lambda z: z.reshape(b, t, RWKV_HEADS, RWKV_HD)
    r, k, v, decay, a = heads(r), heads(k), heads(v), heads(decay), heads(a)
    kk = k * k_k.astype(F32).reshape(RWKV_HEADS, RWKV_HD)
    kk = kk / jnp.maximum(jnp.sqrt(jnp.sum(kk * kk, axis=-1, keepdims=True)), 1e-12)
    k = k * (1.0 + (a - 1.0) * k_a.astype(F32).reshape(RWKV_HEADS, RWKV_HD))
    y = rwkv7_scan(r, decay, k, v, a, kk).reshape(b, t, RWKV_W)
    bonus = (jnp.sum(r * k * r_k.astype(F32), axis=-1, keepdims=True) * v).reshape(b, t, RWKV_W)
    return (head_norm(y, gain, RWKV_HEADS) + bonus) * g


def mlstm_group(cols, gain, conv_w, conv_b, i_b, f_b):
    b, t, _ = cols.shape
    qk, v, o, i_pre, f_pre = _split(cols.astype(F32), MLSTM_SIZES)
    qk = jax.nn.silu(causal_conv(qk, conv_w.astype(F32), conv_b.astype(F32)))
    q, k = jnp.split(qk, 2, axis=-1)
    heads = lambda z: z.reshape(b, t, MLSTM_HEADS, MLSTM_HD)
    h = mlstm_chunkwise(heads(q), heads(k), heads(v), i_pre + i_b.astype(F32),
                        jax.nn.log_sigmoid(f_pre + f_b.astype(F32)))
    return head_norm(h, gain, MLSTM_HEADS) * jax.nn.sigmoid(o)


def setup_inputs(seed: int = 0) -> dict:
    key = jax.random.key(seed)
    ks = jax.random.split(key, 26)
    nrm = lambda k, shape, scale: jax.random.normal(k, shape, F32) * scale
    L, D = DEPTH, D_MODEL
    return {
        'x': nrm(ks[0], (BATCH, SEQ, D), 1.0),
        'c': nrm(ks[1], (BATCH, D), 1.0),
        'ada_w': nrm(ks[2], (L, D, 6 * D), 0.5 * D ** -0.5),
        'ada_b': nrm(ks[3], (L, 6 * D), 0.02),
        'norm_mix': 1.0 + nrm(ks[4], (L, D), 0.02),
        'norm_ffn': 1.0 + nrm(ks[5], (L, D), 0.02),
        'w_in': nrm(ks[6], (L, D, D_IN), D ** -0.5),
        'mix_gn': 1.0 + nrm(ks[7], (L, D_MIX), 0.02),
        'rwkv_mu': jax.random.uniform(ks[8], (L, RWKV_IN), F32, 0.0, 1.0),
        'rwkv_w0': jnp.linspace(-6.0, -1.0, RWKV_W, dtype=F32)[None, :] + nrm(ks[9], (L, RWKV_W), 0.1),
        'rwkv_w2': nrm(ks[10], (L, RWKV_DECAY_LORA, RWKV_W), 0.5 * RWKV_DECAY_LORA ** -0.5),
        'rwkv_a0': nrm(ks[11], (L, RWKV_W), 0.1),
        'rwkv_a2': nrm(ks[12], (L, RWKV_ICL_LORA, RWKV_W), 0.5 * RWKV_ICL_LORA ** -0.5),
        'rwkv_g2': nrm(ks[13], (L, RWKV_GATE_LORA, RWKV_W), RWKV_GATE_LORA ** -0.5),
        'rwkv_k_k': 1.0 + nrm(ks[14], (L, RWKV_W), 0.05),
        'rwkv_k_a': 1.0 + nrm(ks[15], (L, RWKV_W), 0.05),
        'rwkv_r_k': nrm(ks[16], (L, RWKV_HEADS, RWKV_HD), 0.1),
        'mlstm_conv_w': nrm(ks[17], (L, MLSTM_CONV, 2 * MLSTM_W), MLSTM_CONV ** -0.5),
        'mlstm_conv_b': nrm(ks[18], (L, 2 * MLSTM_W), 0.02),
        'mlstm_i_b': nrm(ks[19], (L, MLSTM_HEADS), 0.1),
        'mlstm_f_b': jnp.linspace(3.0, 6.0, MLSTM_HEADS, dtype=F32)[None, :] + nrm(ks[20], (L, MLSTM_HEADS), 0.1),
        'w_out': nrm(ks[21], (L, D_MIX, D), D_MIX ** -0.5),
        'ffn_w_in': nrm(ks[22], (L, D, 2 * D_FF), D ** -0.5),
        'ffn_w_out': nrm(ks[23], (L, D_FF, D), D_FF ** -0.5),
        'final_norm': 1.0 + nrm(ks[24], (D,), 0.02),
    }


def reference(x, c, ada_w, ada_b, norm_mix, norm_ffn, w_in, mix_gn, rwkv_mu, rwkv_w0, rwkv_w2,
              rwkv_a0, rwkv_a2, rwkv_g2, rwkv_k_k, rwkv_k_a, rwkv_r_k, mlstm_conv_w, mlstm_conv_b,
              mlstm_i_b, mlstm_f_b, w_out, ffn_w_in, ffn_w_out, final_norm):
    cond = jax.nn.silu(c)
    for l in range(DEPTH):
        mod = cond @ ada_w[l] + ada_b[l]
        sh_m, sc_m, g_m, sh_f, sc_f, g_f = [z[:, None, :] for z in jnp.split(mod, 6, axis=-1)]
        h = rms_norm(x, norm_mix[l]) * (1.0 + sc_m) + sh_m
        proj = h @ w_in[l]
        ret_cols, rwkv_cols, mlstm_cols = _split(proj, (RET_IN, RWKV_IN, MLSTM_IN))
        gn_ret, gn_rwkv, gn_mlstm = _split(mix_gn[l], (RET_W, RWKV_W, MLSTM_W))
        y_ret = retention_group(ret_cols, gn_ret)
        y_rwkv = rwkv_group(rwkv_cols, gn_rwkv, rwkv_mu[l], rwkv_w0[l], rwkv_w2[l], rwkv_a0[l], rwkv_a2[l],
                            rwkv_g2[l], rwkv_k_k[l], rwkv_k_a[l], rwkv_r_k[l])
        y_mlstm = mlstm_group(mlstm_cols, gn_mlstm, mlstm_conv_w[l], mlstm_conv_b[l], mlstm_i_b[l], mlstm_f_b[l])
        mixed = jnp.concatenate([y_ret, y_rwkv, y_mlstm], axis=-1).astype(x.dtype)
        x = x + g_m * (mixed @ w_out[l])
        h = rms_norm(x, norm_ffn[l]) * (1.0 + sc_f) + sh_f
        gate, up = jnp.split(h @ ffn_w_in[l], 2, axis=-1)
        x = x + g_f * ((jax.nn.silu(gate) * up) @ ffn_w_out[l])
    return rms_norm(x, final_norm)
```

```python
import functools

import numpy as np
import jax
import jax.numpy as jnp
from jax import lax
from jax.experimental import pallas as pl
from jax.experimental.pallas import tpu as pltpu

F32 = jnp.float32
BF16 = jnp.bfloat16

D_MODEL = 1024
CHUNK = 64
RET_HEADS, RET_HD = 4, 64
RET_W = RET_HEADS * RET_HD
RWKV_HEADS, RWKV_HD = 6, 64
RWKV_W = RWKV_HEADS * RWKV_HD
MLSTM_HEADS, MLSTM_HD = 4, 96
MLSTM_W = MLSTM_HEADS * MLSTM_HD
D_MIX = RET_W + RWKV_W + MLSTM_W
RWKV_DECAY_LORA, RWKV_ICL_LORA, RWKV_GATE_LORA = 64, 64, 128
MLSTM_CONV = 4
D_FF = 2816
ROPE_BASE = 10000.0
RMS_EPS = 1e-6
GN_EPS = 1e-5

RET_IN = 4 * RET_W
RWKV_IN = 3 * RWKV_W + RWKV_DECAY_LORA + RWKV_ICL_LORA + RWKV_GATE_LORA
MLSTM_IN = 4 * MLSTM_W + 2 * MLSTM_HEADS
D_IN = RET_IN + RWKV_IN + MLSTM_IN
LANES = 128
GATE_PAD = LANES
MLSTM_IN_PAD = 4 * MLSTM_W + GATE_PAD
D_IN_PAD = RET_IN + RWKV_IN + MLSTM_IN_PAD
RWKV_OFF = RET_IN
MLSTM_OFF = RET_IN + RWKV_IN
LORA_W = RWKV_DECAY_LORA + RWKV_ICL_LORA
FF_CHUNK = 256

VMEM_LIMIT = 56 * 1024 * 1024


def _bf(x):
    return x.astype(BF16)


def _mm(a, b):
    return jnp.dot(_bf(a), _bf(b), preferred_element_type=F32)


def _mm_nt(a, b):
    return lax.dot_general(_bf(a), _bf(b), (((1,), (1,)), ((), ())), preferred_element_type=F32)


def _mm_tn(a, b):
    return lax.dot_general(_bf(a), _bf(b), (((0,), (0,)), ((), ())), preferred_element_type=F32)


def _split2(x):
    hi = _bf(x)
    lo = _bf(x - hi.astype(F32))
    return hi, lo


def _split3(x):
    hi = _bf(x)
    r = x - hi.astype(F32)
    mid = _bf(r)
    lo = _bf(r - mid.astype(F32))
    return hi, mid, lo


def _cumsum_rows(tril_b, x):
    return sum(jnp.dot(tril_b, p, preferred_element_type=F32) for p in _split3(x))


def _segsum(x, ones_b):
    return sum(jnp.dot(p, ones_b, preferred_element_type=F32) for p in _split2(x))


def _block_diag(x, mask_b, n_heads):
    return jnp.concatenate([_bf(x)] * n_heads, axis=0) * mask_b


def _head_norm(y, gain, ones_b, head_dim):
    inv = 1.0 / head_dim
    mu = _segsum(y, ones_b) * inv
    yc = y - mu
    var = _segsum(yc * yc, ones_b) * inv
    return yc * lax.rsqrt(var + GN_EPS) * gain


def _log_sigmoid(x):
    return jnp.minimum(x, 0.0) - jnp.log(1.0 + jnp.exp(-jnp.abs(x)))


def _silu(x):
    return x * jax.nn.sigmoid(x)


def _rms(x):
    return x * lax.rsqrt(jnp.mean(x * x, axis=-1, keepdims=True) + RMS_EPS)


def _shift_rows(prev, cur, k, row):
    z = jnp.where(row >= CHUNK - k, prev, cur)
    return pltpu.roll(z, k, 0)


def _mod_kernel(c_ref, w_ref, b_ref, o_ref):
    cond = _silu(c_ref[...])
    o_ref[0] = _mm(cond, w_ref[0]) + b_ref[0]


def _modulation(c, ada_w, ada_b):
    depth, d, n = ada_w.shape
    b = c.shape[0]
    tn = 1536
    return pl.pallas_call(
        _mod_kernel,
        grid=(depth, n // tn),
        in_specs=[
            pl.BlockSpec((b, d), lambda l, j: (0, 0)),
            pl.BlockSpec((1, d, tn), lambda l, j: (l, 0, j)),
            pl.BlockSpec((1, 1, tn), lambda l, j: (l, 0, j)),
        ],
        out_specs=pl.BlockSpec((1, b, tn), lambda l, j: (l, 0, j)),
        out_shape=jax.ShapeDtypeStruct((depth, b, n), F32),
        compiler_params=pltpu.CompilerParams(dimension_semantics=("arbitrary", "arbitrary"),
                                             vmem_limit_bytes=VMEM_LIMIT),
        name="adaln_modulation",
    )(c, ada_w, ada_b.reshape(depth, 1, n))


def _inproj_kernel(x_ref, sh_ref, sc_ref, g_ref, w_ref, o_ref):
    h = _rms(x_ref[...]) * g_ref[...]
    h = _bf(h * (1.0 + sc_ref[0]) + sh_ref[0])
    n_out = o_ref.shape[1]
    step = 1024
    for n0 in range(0, n_out, step):
        o_ref[:, n0:n0 + step] = jnp.dot(h, w_ref[:, n0:n0 + step], preferred_element_type=F32)


def _mod_spec(col):
    return lambda tiles_per_batch: pl.BlockSpec((1, 1, D_MODEL), lambda i: (i // tiles_per_batch, 0, col))


def _in_projection(x2, mod3, gain, w_b, seq):
    n_tok, d = x2.shape
    tm = 256
    tpb = seq // tm
    n_out = w_b.shape[1]
    return pl.pallas_call(
        _inproj_kernel,
        grid=(n_tok // tm,),
        in_specs=[
            pl.BlockSpec((tm, d), lambda i: (i, 0)),
            _mod_spec(0)(tpb),
            _mod_spec(1)(tpb),
            pl.BlockSpec((1, d), lambda i: (0, 0)),
            pl.BlockSpec((d, n_out), lambda i: (0, 0)),
        ],
        out_specs=pl.BlockSpec((tm, n_out), lambda i: (i, 0)),
        out_shape=jax.ShapeDtypeStruct((n_tok, n_out), F32),
        compiler_params=pltpu.CompilerParams(dimension_semantics=("arbitrary",),
                                             vmem_limit_bytes=VMEM_LIMIT),
        name="in_projection",
    )(x2, mod3, mod3, gain, w_b)


def _retention_chunk(proj_ref, cos, sin, dec_ref, cdec, m64f, m64b, tril, gain, state_ref):
    w = RET_W
    lane = lax.broadcasted_iota(jnp.int32, (CHUNK, w), 1)
    first_half = (lane & (RET_HD - 1)) < RET_HD // 2

    def rope(z):
        swapped = jnp.where(first_half, pltpu.roll(z, w - RET_HD // 2, 1), pltpu.roll(z, RET_HD // 2, 1))
        return z * cos + swapped * sin

    q = rope(proj_ref[:, 0:w])
    k = rope(proj_ref[:, w:2 * w]) * (RET_HD ** -0.5)
    v = proj_ref[:, 2 * w:3 * w]
    g = proj_ref[:, 3 * w:4 * w]
    mb = m64b[0:w, 0:w]
    state = state_ref[...]
    scores = _mm_nt(q, _block_diag(k, mb, RET_HEADS)) * dec_ref[0]
    y = _mm(scores, _block_diag(v, mb, RET_HEADS)) + _mm(q, state) * dec_ref[1]
    state_ref[...] = state * cdec + m64f[0:w, 0:w] * _mm_tn(k * dec_ref[2], v)
    return _head_norm(y, gain, mb, RET_HD) * _silu(g)


def _rwkv_chunk(proj_ref, m64f, m64b, tril, stril, mu, p384, w2a2, g2, state_ref, prev_ref):
    w = RWKV_W
    row = lax.broadcasted_iota(jnp.int32, (CHUNK, 1), 0)
    cols = proj_ref[:, RWKV_OFF:RWKV_OFF + RWKV_IN]
    prev = _shift_rows(prev_ref[...], cols, 1, row)
    prev_ref[...] = cols
    xs = cols + (prev - cols) * mu
    r = xs[:, 0:w]
    k = xs[:, w:2 * w]
    v = xs[:, 2 * w:3 * w]
    lora = xs[:, 3 * w:3 * w + LORA_W]
    gl = xs[:, 3 * w + LORA_W:RWKV_IN]
    w0, a0, k_k, k_a, r_k, gain = (p384[i:i + 1, :] for i in range(6))

    lane = lax.broadcasted_iota(jnp.int32, (CHUNK, LORA_W), 1)
    wa = _mm(jnp.where(lane < RWKV_DECAY_LORA, jnp.tanh(lora), lora), w2a2)
    log_decay = -jnp.exp(_log_sigmoid(w0 + wa[:, 0:w]) - 0.5)
    a = jax.nn.sigmoid(a0 + wa[:, w:2 * w])
    g = _mm(jax.nn.sigmoid(gl), g2)

    kk = k * k_k
    kk = kk / jnp.maximum(jnp.sqrt(_segsum(kk * kk, m64b)), 1e-12)
    k = k * (1.0 + (a - 1.0) * k_a)
    bonus = _segsum(r * k * r_k, m64b) * v

    c = _cumsum_rows(_bf(tril[:, 0:CHUNK]), log_decay)
    c_last = c[CHUNK - 1:CHUNK, :]
    p_inv = jnp.exp(-c)
    p_end = jnp.exp(c_last - c)
    rh = r * jnp.exp(c)
    bh = -kk * jnp.exp(c - log_decay)
    ah = kk * a * p_inv
    kh = k * p_inv

    state = state_ref[...]
    lhs = jnp.concatenate([bh, rh], axis=0)
    s_a = _mm_nt(lhs, _block_diag(ah, m64b, RWKV_HEADS))
    s_k = _mm_nt(lhs, _block_diag(kh, m64b, RWKV_HEADS))
    n_mat = s_a[0:CHUNK] * stril
    a_bk = s_k[0:CHUNK] * stril
    s_ra = s_a[CHUNK:] * tril
    s_rk = s_k[CHUNK:] * tril
    from_state = _mm_nt(lhs, state)
    bd_v = _block_diag(v, m64b, RWKV_HEADS)

    u = from_state[0:CHUNK] + _mm(a_bk, bd_v)
    n_pow = n_mat
    steps = CHUNK.bit_length() - 1
    for j in range(steps):
        u = u + _mm(n_pow, _block_diag(u, m64b, RWKV_HEADS))
        if j + 1 < steps:
            n_pow = _mm(n_pow, _block_diag(n_pow, m64b, RWKV_HEADS))

    y = from_state[CHUNK:] + _mm(s_ra, _block_diag(u, m64b, RWKV_HEADS)) + _mm(s_rk, bd_v)
    upd = _mm_tn(jnp.concatenate([u, v], axis=0), jnp.concatenate([kk * a * p_end, k * p_end], axis=0))
    state_ref[...] = state * jnp.exp(c_last) + m64f * upd
    return (_head_norm(y, gain, m64b, RWKV_HD) + bonus) * g


def _mlstm_chunk(proj_ref, m96f, m96b, m96xb, tril, stril, p384, convw, convb, gbias,
                 c_ref, n_ref, m_ref, prev_ref):
    w = MLSTM_W
    nh = MLSTM_HEADS
    sw = nh * CHUNK
    row = lax.broadcasted_iota(jnp.int32, (CHUNK, 1), 0)
    lane_g = lax.broadcasted_iota(jnp.int32, (1, GATE_PAD), 1)
    lane_s = lax.broadcasted_iota(jnp.int32, (1, sw), 1)
    lane_w = lax.broadcasted_iota(jnp.int32, (1, w), 1)
    head_s = [(lane_s >= h * CHUNK) & (lane_s < (h + 1) * CHUNK) for h in range(nh)]
    head_w = [(lane_w >= h * MLSTM_HD) & (lane_w < (h + 1) * MLSTM_HD) for h in range(nh)]
    gain = p384[6:7, :]

    def col(x, h):
        return jnp.sum(jnp.where(lane_g == h, x, 0.0), axis=1, keepdims=True)

    def spread(cols, heads):
        out = jnp.where(heads[0], cols[0], 0.0)
        for h in range(1, nh):
            out = jnp.where(heads[h], cols[h], out)
        return out

    off = MLSTM_OFF
    qk_pre = proj_ref[:, off:off + 2 * w]
    prev = prev_ref[...]
    prev_ref[...] = qk_pre
    conv = convb + qk_pre * convw[MLSTM_CONV - 1:MLSTM_CONV, :]
    for j in range(MLSTM_CONV - 1):
        conv = conv + _shift_rows(prev, qk_pre, MLSTM_CONV - 1 - j, row) * convw[j:j + 1, :]
    qk = _silu(conv)
    q = qk[:, 0:w]
    k = qk[:, w:2 * w] * (MLSTM_HD ** -0.5)
    v = proj_ref[:, off + 2 * w:off + 3 * w]
    o = proj_ref[:, off + 3 * w:off + 4 * w]
    gates = proj_ref[:, off + 4 * w:off + 4 * w + GATE_PAD] + gbias
    i_pre = gates
    log_f = _log_sigmoid(pltpu.roll(gates, GATE_PAD - nh, 1))
    bcum = _cumsum_rows(_bf(tril[:, 0:CHUNK]), log_f)
    m_prev = m_ref[0:1, :]

    bcum_c = [col(bcum, h) for h in range(nh)]
    src_c = [col(i_pre - bcum, h) for h in range(nh)]
    eye = tril[:, 0:sw] - stril[:, 0:sw]
    src_row = jnp.sum(spread(src_c, head_s) * eye, axis=0, keepdims=True)
    d_mat = jnp.where(tril[:, 0:sw] > 0.0, spread(bcum_c, head_s) + src_row, -jnp.inf)
    a_c = [bcum_c[h] + col(m_prev, h) for h in range(nh)]
    m_c = [jnp.maximum(a_c[h], jnp.max(jnp.where(head_s[h], d_mat, -jnp.inf), axis=1, keepdims=True))
           for h in range(nh)]
    inter_c = [jnp.exp(a_c[h] - m_c[h]) for h in range(nh)]

    s = _mm_nt(q, _block_diag(k, m96xb, nh)) * jnp.exp(d_mat - spread(m_c, head_s))
    c_state = c_ref[...]
    n_state = n_ref[0:1, :]
    num = _mm(s, _block_diag(v, m96xb, nh)) + spread(inter_c, head_w) * _mm(q, c_state)
    qn = q * n_state
    den_c = [jnp.sum(jnp.where(head_s[h], s, 0.0), axis=1, keepdims=True)
             + inter_c[h] * jnp.sum(jnp.where(head_w[h], qn, 0.0), axis=1, keepdims=True) for h in range(nh)]
    floor_c = [jnp.maximum(jnp.abs(den_c[h]), jnp.exp(-m_c[h])) for h in range(nh)]
    hc = num / spread(floor_c, head_w)

    b_last = bcum[CHUNK - 1:CHUNK, :]
    g_s = b_last - bcum + i_pre
    m_new = jnp.maximum(b_last + m_prev, jnp.max(g_s, axis=0, keepdims=True))
    ws = jnp.exp(g_s - m_new)
    carry = jnp.exp(b_last + m_prev - m_new)
    kw = k * spread([col(ws, h) for h in range(nh)], head_w)
    carry_w = spread([col(carry, h) for h in range(nh)], head_w)
    c_ref[...] = c_state * carry_w + m96f * _mm_tn(kw, v)
    n_ref[0:1, :] = n_state * carry_w + jnp.sum(kw, axis=0, keepdims=True)
    m_ref[0:1, :] = m_new
    return _head_norm(hc, gain, m96b, MLSTM_HD) * jax.nn.sigmoid(o)


def _strict_lower(width):
    src = lax.broadcasted_iota(jnp.int32, (CHUNK, width), 1) & (CHUNK - 1)
    dst = lax.broadcasted_iota(jnp.int32, (CHUNK, width), 0)
    return jnp.where(dst > src, 1.0, 0.0).astype(F32)


def _mixer_kernel(proj_ref, cos_ref, sin_ref, dec_ref, cdec_ref, m64f_ref, m64b_ref, m96f_ref, m96b_ref,
                  m96xb_ref, tril_ref, gret_ref, mu_ref, p384_ref, w2a2_ref, g2_ref, convw_ref, convb_ref,
                  gbias_ref, out_ref,
                  s_ret, s_rwkv, prev_rwkv, c_ml, n_ml, m_ml, prev_qk):
    @pl.when(pl.program_id(1) == 0)
    def _():
        for ref in (s_ret, s_rwkv, prev_rwkv, c_ml, n_ml, m_ml, prev_qk):
            ref[...] = jnp.zeros(ref.shape, ref.dtype)

    tril = tril_ref[...]
    stril = _strict_lower(RWKV_W)
    m64f = m64f_ref[...]
    m64b = m64b_ref[...]
    y_ret = _retention_chunk(proj_ref, cos_ref[...], sin_ref[...], dec_ref, cdec_ref[...], m64f, m64b, tril,
                             gret_ref[...], s_ret)
    out_ref[:, 0:RET_W] = y_ret.astype(out_ref.dtype)
    y_rwkv = _rwkv_chunk(proj_ref, m64f, m64b, tril, stril, mu_ref[...], p384_ref[...], w2a2_ref[...],
                         g2_ref[...], s_rwkv, prev_rwkv)
    out_ref[:, RET_W:RET_W + RWKV_W] = y_rwkv.astype(out_ref.dtype)
    y_ml = _mlstm_chunk(proj_ref, m96f_ref[...], m96b_ref[...], m96xb_ref[...], tril, stril, p384_ref[...],
                        convw_ref[...], convb_ref[...], gbias_ref[...], c_ml, n_ml, m_ml, prev_qk)
    out_ref[:, RET_W + RWKV_W:D_MIX] = y_ml.astype(out_ref.dtype)


def _mixer_constants(seq):
    h = jnp.arange(RET_HEADS, dtype=F32)
    log_g = jnp.log1p(-jnp.exp2(-5.0 - h))
    idx = jnp.arange(CHUNK, dtype=F32)
    diff = idx[:, None] - idx[None, :]
    intra = jnp.where(diff >= 0, jnp.exp(log_g[:, None, None] * jnp.maximum(diff, 0.0)), 0.0)
    intra = jnp.transpose(intra, (1, 0, 2)).reshape(CHUNK, RET_HEADS * CHUNK)
    q_decay = jnp.exp(log_g[:, None] * (idx + 1.0))
    k_decay = jnp.exp(log_g[:, None] * (CHUNK - 1.0 - idx))
    rep = lambda z: jnp.repeat(z.T, RET_HD, axis=1)
    dec = jnp.stack([intra, rep(q_decay), rep(k_decay)])
    cdec = jnp.repeat(jnp.exp(log_g * CHUNK), RET_HD)[None, :]

    half = RET_HD // 2
    inv = ROPE_BASE ** (-jnp.arange(0, RET_HD, 2, dtype=F32) / RET_HD)
    ang = jnp.arange(seq, dtype=F32)[:, None] * inv[None, :]
    cos = jnp.tile(jnp.cos(ang), (1, 2 * RET_HEADS))
    sin_h = jnp.concatenate([-jnp.sin(ang), jnp.sin(ang)], axis=1)
    sin = jnp.tile(sin_h, (1, RET_HEADS))
    del half

    l64 = np.arange(RWKV_W) // 64
    l96 = np.arange(MLSTM_W) // MLSTM_HD
    m64 = (l64[:, None] == l64[None, :]).astype(np.float32)
    m96 = (l96[:, None] == l96[None, :]).astype(np.float32)
    m96x = (l64[:MLSTM_HEADS * CHUNK, None] == l96[None, :]).astype(np.float32)
    t = np.arange(CHUNK)
    tril = np.tile((t[:, None] >= t[None, :]).astype(np.float32), (1, RWKV_HEADS))
    return dict(cos=cos, sin=sin, dec=dec, cdec=cdec,
                m64f=jnp.asarray(m64), m64b=jnp.asarray(m64, BF16),
                m96f=jnp.asarray(m96), m96b=jnp.asarray(m96, BF16),
                m96xb=jnp.asarray(m96x, BF16), tril=jnp.asarray(tril))


def _mixers(proj, consts, lp, batch, seq):
    nc = seq // CHUNK
    full = lambda a: pl.BlockSpec(a.shape, lambda b, c: (0,) * a.ndim)
    const_inputs = [consts["dec"], consts["cdec"], consts["m64f"], consts["m64b"], consts["m96f"],
                    consts["m96b"], consts["m96xb"], consts["tril"]]
    layer_inputs = [lp["gain_ret"], lp["mu"], lp["p384"], lp["w2a2"], lp["g2"], lp["convw"], lp["convb"],
                    lp["gbias"]]
    return pl.pallas_call(
        _mixer_kernel,
        grid=(batch, nc),
        in_specs=[pl.BlockSpec((CHUNK, D_IN_PAD), lambda b, c: (b * nc + c, 0)),
                  pl.BlockSpec((CHUNK, RET_W), lambda b, c: (c, 0)),
                  pl.BlockSpec((CHUNK, RET_W), lambda b, c: (c, 0))]
                 + [full(a) for a in const_inputs + layer_inputs],
        out_specs=pl.BlockSpec((CHUNK, D_MIX), lambda b, c: (b * nc + c, 0)),
        out_shape=jax.ShapeDtypeStruct((batch * seq, D_MIX), BF16),
        scratch_shapes=[
            pltpu.VMEM((RET_W, RET_W), F32),
            pltpu.VMEM((RWKV_W, RWKV_W), F32),
            pltpu.VMEM((CHUNK, RWKV_IN), F32),
            pltpu.VMEM((MLSTM_W, MLSTM_W), F32),
            pltpu.VMEM((8, MLSTM_W), F32),
            pltpu.VMEM((8, GATE_PAD), F32),
            pltpu.VMEM((CHUNK, 2 * MLSTM_W), F32),
        ],
        compiler_params=pltpu.CompilerParams(dimension_semantics=("arbitrary", "arbitrary"),
                                             vmem_limit_bytes=VMEM_LIMIT),
        name="mixers",
    )(proj, consts["cos"], consts["sin"], *const_inputs, *layer_inputs)


def _out_ffn_kernel(x_ref, mix_ref, gm_ref, shf_ref, scf_ref, gf_ref, wout_ref, gain_ref, wi_ref, wo_ref,
                    fin_ref, o_ref, *, final):
    x = x_ref[...] + gm_ref[0] * jnp.dot(mix_ref[...], wout_ref[...], preferred_element_type=F32)
    h = _rms(x) * gain_ref[...]
    h = _bf(h * (1.0 + scf_ref[0]) + shf_ref[0])
    acc = jnp.zeros(x.shape, F32)
    for f0 in range(0, D_FF, FF_CHUNK):
        gate = jnp.dot(h, wi_ref[:, f0:f0 + FF_CHUNK], preferred_element_type=F32)
        up = jnp.dot(h, wi_ref[:, D_FF + f0:D_FF + f0 + FF_CHUNK], preferred_element_type=F32)
        acc = acc + jnp.dot(_bf(_silu(gate) * up), wo_ref[f0:f0 + FF_CHUNK, :], preferred_element_type=F32)
    x = x + gf_ref[0] * acc
    if final:
        x = _rms(x) * fin_ref[...]
    o_ref[...] = x


def _out_ffn(x2, mixed, mod3, w_out_b, gain, wi_b, wo_b, final_gain, seq, final):
    n_tok, d = x2.shape
    tm = 256
    tpb = seq // tm
    full = lambda a: pl.BlockSpec(a.shape, lambda i: (0,) * a.ndim)
    return pl.pallas_call(
        functools.partial(_out_ffn_kernel, final=final),
        grid=(n_tok // tm,),
        in_specs=[
            pl.BlockSpec((tm, d), lambda i: (i, 0)),
            pl.BlockSpec((tm, D_MIX), lambda i: (i, 0)),
            _mod_spec(2)(tpb), _mod_spec(3)(tpb), _mod_spec(4)(tpb), _mod_spec(5)(tpb),
            full(w_out_b), full(gain), full(wi_b), full(wo_b), full(final_gain),
        ],
        out_specs=pl.BlockSpec((tm, d), lambda i: (i, 0)),
        out_shape=jax.ShapeDtypeStruct((n_tok, d), F32),
        compiler_params=pltpu.CompilerParams(dimension_semantics=("arbitrary",),
                                             vmem_limit_bytes=VMEM_LIMIT),
        name="out_ffn",
    )(x2, mixed, mod3, mod3, mod3, mod3, w_out_b, gain, wi_b, wo_b, final_gain)


def _layer_params(l, norm_mix, mix_gn, rwkv_mu, rwkv_w0, rwkv_w2, rwkv_a0, rwkv_a2, rwkv_g2, rwkv_k_k,
                  rwkv_k_a, rwkv_r_k, mlstm_conv_w, mlstm_conv_b, mlstm_i_b, mlstm_f_b):
    gn = mix_gn[l]
    zeros = jnp.zeros((RWKV_W,), F32)
    p384 = jnp.stack([rwkv_w0[l], rwkv_a0[l], rwkv_k_k[l], rwkv_k_a[l], rwkv_r_k[l].reshape(RWKV_W),
                      gn[RET_W:RET_W + RWKV_W], gn[RET_W + RWKV_W:], zeros])
    w2a2 = jnp.zeros((LORA_W, 2 * RWKV_W), F32)
    w2a2 = w2a2.at[:RWKV_DECAY_LORA, :RWKV_W].set(rwkv_w2[l]).at[RWKV_DECAY_LORA:, RWKV_W:].set(rwkv_a2[l])
    gbias = jnp.zeros((1, GATE_PAD), F32)
    gbias = gbias.at[0, :MLSTM_HEADS].set(mlstm_i_b[l]).at[0, MLSTM_HEADS:2 * MLSTM_HEADS].set(mlstm_f_b[l])
    convw = jnp.zeros((8, 2 * MLSTM_W), F32).at[:MLSTM_CONV].set(mlstm_conv_w[l])
    return dict(gain_ret=gn[None, :RET_W], mu=rwkv_mu[l][None, :], p384=p384, w2a2=_bf(w2a2),
                g2=_bf(rwkv_g2[l]), convw=convw, convb=mlstm_conv_b[l][None, :], gbias=gbias)


def kernel(x, c, ada_w, ada_b, norm_mix, norm_ffn, w_in, mix_gn, rwkv_mu, rwkv_w0, rwkv_w2, rwkv_a0, rwkv_a2, rwkv_g2, rwkv_k_k, rwkv_k_a, rwkv_r_k, mlstm_conv_w, mlstm_conv_b, mlstm_i_b, mlstm_f_b, w_out, ffn_w_in, ffn_w_out, final_norm):
    batch, seq, d = x.shape
    depth = ada_w.shape[0]
    assert d == D_MODEL and seq % 256 == 0
    consts = _mixer_constants(seq)
    mod = _modulation(c, ada_w, ada_b)
    x2 = x.reshape(batch * seq, d)
    final_gain = final_norm[None, :]
    for l in range(depth):
        mod3 = mod[l].reshape(batch, 1, 6 * d)
        w_in_b = _bf(jnp.pad(w_in[l], ((0, 0), (0, D_IN_PAD - D_IN))))
        lp = _layer_params(l, norm_mix, mix_gn, rwkv_mu, rwkv_w0, rwkv_w2, rwkv_a0, rwkv_a2, rwkv_g2,
                           rwkv_k_k, rwkv_k_a, rwkv_r_k, mlstm_conv_w, mlstm_conv_b, mlstm_i_b, mlstm_f_b)
        proj = _in_projection(x2, mod3, norm_mix[l][None, :], w_in_b, seq)
        mixed = _mixers(proj, consts, lp, batch, seq)
        x2 = _out_ffn(x2, mixed, mod3, _bf(w_out[l]), norm_ffn[l][None, :], _bf(ffn_w_in[l]),
                      _bf(ffn_w_out[l]), final_gain, seq, final=(l == depth - 1))
    return x2.reshape(batch, seq, d)
```

```python
import functools

import numpy as np
import jax
import jax.numpy as jnp
from jax import lax
from jax.experimental import pallas as pl
from jax.experimental.pallas import tpu as pltpu

F32 = jnp.float32
BF16 = jnp.bfloat16

D_MODEL = 1024
CHUNK = 64
RET_HEADS, RET_HD = 4, 64
RET_W = RET_HEADS * RET_HD
RWKV_HEADS, RWKV_HD = 6, 64
RWKV_W = RWKV_HEADS * RWKV_HD
MLSTM_HEADS, MLSTM_HD = 4, 96
MLSTM_W = MLSTM_HEADS * MLSTM_HD
D_MIX = RET_W + RWKV_W + MLSTM_W
RWKV_DECAY_LORA, RWKV_ICL_LORA, RWKV_GATE_LORA = 64, 64, 128
MLSTM_CONV = 4
D_FF = 2816
ROPE_BASE = 10000.0
RMS_EPS = 1e-6
GN_EPS = 1e-5

RET_IN = 4 * RET_W
RWKV_IN = 3 * RWKV_W + RWKV_DECAY_LORA + RWKV_ICL_LORA + RWKV_GATE_LORA
MLSTM_IN = 4 * MLSTM_W + 2 * MLSTM_HEADS
D_IN = RET_IN + RWKV_IN + MLSTM_IN
TILE = 128
SUB = 8
MLSTM_WP = MLSTM_HEADS * TILE
MLSTM_IN_PAD = 4 * MLSTM_WP + TILE
D_IN_PAD = RET_IN + RWKV_IN + MLSTM_IN_PAD
D_MIX_PAD = RET_W + RWKV_W + MLSTM_WP
RWKV_OFF = RET_IN
MLSTM_OFF = RET_IN + RWKV_IN
LORA_W = RWKV_DECAY_LORA + RWKV_ICL_LORA
FF_CHUNK = 256
MIX_BB = 4
MIX_CHUNKS = 1

VMEM_LIMIT = 56 * 1024 * 1024
_DONE = object()


def _bf(x):
    return x.astype(BF16)


def _mm(a, b):
    return jnp.dot(_bf(a), _bf(b), preferred_element_type=F32)


def _mm_nt(a, b):
    return lax.dot_general(_bf(a), _bf(b), (((1,), (1,)), ((), ())), preferred_element_type=F32)


def _mm_tn(a, b):
    return lax.dot_general(_bf(a), _bf(b), (((0,), (0,)), ((), ())), preferred_element_type=F32)


def _split2(x):
    hi = _bf(x)
    lo = _bf(x - hi.astype(F32))
    return hi, lo


def _split3(x):
    hi = _bf(x)
    r = x - hi.astype(F32)
    mid = _bf(r)
    lo = _bf(r - mid.astype(F32))
    return hi, mid, lo


def _tiles(x):
    return [x[:, t * TILE:(t + 1) * TILE] for t in range(x.shape[1] // TILE)]


def _cumsum_rows(tril_b, x):
    w = x.shape[1]
    r = jnp.dot(tril_b, jnp.concatenate(_split3(x), axis=1), preferred_element_type=F32)
    return r[:, 0:w] + r[:, w:2 * w] + r[:, 2 * w:3 * w]


def _segsum64(x, ones2_b):
    rows = x.shape[0]
    parts = [t for p in _split2(x) for t in _tiles(p)]
    r = jnp.dot(jnp.concatenate(parts, axis=0), ones2_b, preferred_element_type=F32)
    nt = len(parts) // 2
    return jnp.concatenate([r[t * rows:(t + 1) * rows] + r[(nt + t) * rows:(nt + t + 1) * rows]
                            for t in range(nt)], axis=1)


def _bd2(y, lo_half):
    yb = _bf(y)
    zero = jnp.zeros_like(yb)
    return jnp.concatenate([jnp.where(lo_half, yb, zero), jnp.where(lo_half, zero, yb)], axis=0)


def _head_norm64(y, gain, ones2_b):
    inv = 1.0 / 64
    mu = _segsum64(y, ones2_b) * inv
    yc = y - mu
    var = _segsum64(yc * yc, ones2_b) * inv
    return yc * lax.rsqrt(var + GN_EPS) * gain


def _log_sigmoid(x):
    return jnp.minimum(x, 0.0) - jnp.log(1.0 + jnp.exp(-jnp.abs(x)))


def _silu(x):
    return x * jax.nn.sigmoid(x)


def _rms(x):
    return x * lax.rsqrt(jnp.mean(x * x, axis=-1, keepdims=True) + RMS_EPS)


def _shift_rows(prev8, cur, k):
    rolled = pltpu.roll(cur, k, 0)
    row8 = lax.broadcasted_iota(jnp.int32, (SUB, 1), 0)
    first = jnp.where(row8 < k, pltpu.roll(prev8, k, 0), rolled[0:SUB])
    return jnp.concatenate([first, rolled[SUB:]], axis=0)


def _mod_kernel(c_ref, w_ref, b_ref, o_ref):
    cond = _silu(c_ref[...])
    o_ref[0] = _mm(cond, w_ref[0]) + b_ref[0]


def _modulation(c, ada_w, ada_b):
    depth, d, n = ada_w.shape
    b = c.shape[0]
    tn = 1536
    return pl.pallas_call(
        _mod_kernel,
        grid=(depth, n // tn),
        in_specs=[
            pl.BlockSpec((b, d), lambda l, j: (0, 0)),
            pl.BlockSpec((1, d, tn), lambda l, j: (l, 0, j)),
            pl.BlockSpec((1, 1, tn), lambda l, j: (l, 0, j)),
        ],
        out_specs=pl.BlockSpec((1, b, tn), lambda l, j: (l, 0, j)),
        out_shape=jax.ShapeDtypeStruct((depth, b, n), F32),
        compiler_params=pltpu.CompilerParams(dimension_semantics=("arbitrary", "arbitrary"),
                                             vmem_limit_bytes=VMEM_LIMIT),
        name="adaln_modulation",
    )(c, ada_w, ada_b.reshape(depth, 1, n))


def _inproj_kernel(x_ref, sh_ref, sc_ref, g_ref, w_ref, o_ref):
    h = _rms(x_ref[...]) * g_ref[...]
    h = _bf(h * (1.0 + sc_ref[0]) + sh_ref[0])
    n_out = o_ref.shape[1]
    step = 1536
    for n0 in range(0, n_out, step):
        o_ref[:, n0:n0 + step] = jnp.dot(h, w_ref[:, n0:n0 + step], preferred_element_type=F32)


def _mod_spec(col, tiles_per_batch):
    return pl.BlockSpec((1, 1, D_MODEL), lambda i: (i // tiles_per_batch, 0, col))


def _in_projection(x2, mod3, gain, w_b, seq):
    n_tok, d = x2.shape
    tm = 256
    tpb = seq // tm
    n_out = w_b.shape[1]
    return pl.pallas_call(
        _inproj_kernel,
        grid=(n_tok // tm,),
        in_specs=[
            pl.BlockSpec((tm, d), lambda i: (i, 0)),
            _mod_spec(0, tpb),
            _mod_spec(1, tpb),
            pl.BlockSpec((1, d), lambda i: (0, 0)),
            pl.BlockSpec((d, n_out), lambda i: (0, 0)),
        ],
        out_specs=pl.BlockSpec((tm, n_out), lambda i: (i, 0)),
        out_shape=jax.ShapeDtypeStruct((n_tok, n_out), F32),
        compiler_params=pltpu.CompilerParams(dimension_semantics=("arbitrary",),
                                             vmem_limit_bytes=VMEM_LIMIT),
        name="in_projection",
    )(x2, mod3, mod3, gain, w_b)


def _retention_chunk(cols, write, cos, sin, dec_ref, cdec, mask2f, ones2b, lo_half, gain, state_ref):
    w = RET_W
    lane = lax.broadcasted_iota(jnp.int32, (CHUNK, w), 1)
    first_half = (lane & (RET_HD - 1)) < RET_HD // 2

    def rope(z):
        swapped = jnp.where(first_half, pltpu.roll(z, w - RET_HD // 2, 1), pltpu.roll(z, RET_HD // 2, 1))
        return z * cos + swapped * sin

    q = rope(cols(0, w))
    k = rope(cols(w, 2 * w)) * (RET_HD ** -0.5)
    v = cols(2 * w, 3 * w)
    g = cols(3 * w, 4 * w)
    intra, q_dec = _tiles(dec_ref[0]), _tiles(dec_ref[1])
    q_t, k_t, v_t, kd_t = _tiles(q), _tiles(k), _tiles(v), _tiles(k * dec_ref[2])
    nt = len(q_t)
    states = [state_ref[t] for t in range(nt)]
    scores = [_mm_nt(q_t[t], _bd2(k_t[t], lo_half)) * intra[t] for t in range(nt)]
    inter = [_mm(q_t[t], states[t]) * q_dec[t] for t in range(nt)]
    yield
    y = jnp.concatenate([_mm(scores[t], _bd2(v_t[t], lo_half)) + inter[t] for t in range(nt)], axis=1)
    for t in range(nt):
        state_ref[t] = states[t] * cdec[:, t * TILE:(t + 1) * TILE] + mask2f * _mm_tn(kd_t[t], v_t[t])
    yield
    mu = _segsum64(y, ones2b) * (1.0 / RET_HD)
    yield
    yc = y - mu
    var = _segsum64(yc * yc, ones2b) * (1.0 / RET_HD)
    write(yc * lax.rsqrt(var + GN_EPS) * gain * _silu(g))


def _rwkv_chunk(cols, write, tril, stril, tril64b, mask2f, ones2b, lo_half, mu, p384, w2a2, g2, state_ref,
                prev_ref):
    w = RWKV_W
    x = cols(RWKV_OFF, RWKV_OFF + RWKV_IN)
    prev = _shift_rows(prev_ref[...], x, 1)
    prev_ref[...] = x[CHUNK - SUB:CHUNK]
    xs = x + (prev - x) * mu
    r = xs[:, 0:w]
    k = xs[:, w:2 * w]
    v = xs[:, 2 * w:3 * w]
    lora = xs[:, 3 * w:3 * w + LORA_W]
    gl = xs[:, 3 * w + LORA_W:RWKV_IN]
    w0, a0, k_k, k_a, r_k, gain = (p384[i:i + 1, :] for i in range(6))

    lane = lax.broadcasted_iota(jnp.int32, (CHUNK, LORA_W), 1)
    wa = _mm(jnp.where(lane < RWKV_DECAY_LORA, jnp.tanh(lora), lora), w2a2)
    g = _mm(jax.nn.sigmoid(gl), g2)
    kk = k * k_k
    kk_sq = _segsum64(kk * kk, ones2b)
    yield
    log_decay = -jnp.exp(_log_sigmoid(w0 + wa[:, 0:w]) - 0.5)
    a = jax.nn.sigmoid(a0 + wa[:, w:2 * w])
    kk = kk / jnp.maximum(jnp.sqrt(kk_sq), 1e-12)
    k = k * (1.0 + (a - 1.0) * k_a)
    bonus = _segsum64(r * k * r_k, ones2b) * v
    c = _cumsum_rows(tril64b, log_decay)
    yield
    c_last = c[CHUNK - 1:CHUNK, :]
    p_inv = jnp.exp(-c)
    p_end = jnp.exp(c_last - c)
    p_last = jnp.exp(c_last)
    rh = r * jnp.exp(c)
    bh = -kk * jnp.exp(c - log_decay)
    ka = kk * a
    lhs_t = _tiles(jnp.concatenate([bh, rh], axis=0))
    upd_t = _tiles(jnp.concatenate([ka * p_end, k * p_end], axis=0))
    ah_t, kh_t, v_t = _tiles(ka * p_inv), _tiles(k * p_inv), _tiles(v)
    nt = len(v_t)
    tiles = range(nt)

    states = [state_ref[t] for t in tiles]
    s_a = [_mm_nt(lhs_t[t], _bd2(ah_t[t], lo_half)) for t in tiles]
    s_k = [_mm_nt(lhs_t[t], _bd2(kh_t[t], lo_half)) for t in tiles]
    from_state = [_mm_nt(lhs_t[t], states[t]) for t in tiles]
    bd_v = [_bd2(v_t[t], lo_half) for t in tiles]
    yield
    u = [from_state[t][0:CHUNK] + _mm(s_k[t][0:CHUNK] * stril, bd_v[t]) for t in tiles]
    n_pow = [s_a[t][0:CHUNK] * stril for t in tiles]
    y_part = [from_state[t][CHUNK:] + _mm(s_k[t][CHUNK:] * tril, bd_v[t]) for t in tiles]
    yield
    steps = CHUNK.bit_length() - 1
    for j in range(steps):
        u = [u[t] + _mm(n_pow[t], _bd2(u[t], lo_half)) for t in tiles]
        if j + 1 < steps:
            n_pow = [_mm(n_pow[t], _bd2(n_pow[t], lo_half)) for t in tiles]
        yield
    y = jnp.concatenate([y_part[t] + _mm(s_a[t][CHUNK:] * tril, _bd2(u[t], lo_half)) for t in tiles], axis=1)
    for t in tiles:
        upd = _mm_tn(jnp.concatenate([u[t], v_t[t]], axis=0), upd_t[t])
        state_ref[t] = states[t] * p_last[:, t * TILE:(t + 1) * TILE] + mask2f * upd
    yield
    mean = _segsum64(y, ones2b) * (1.0 / RWKV_HD)
    yield
    yc = y - mean
    var = _segsum64(yc * yc, ones2b) * (1.0 / RWKV_HD)
    write((yc * lax.rsqrt(var + GN_EPS) * gain + bonus) * g)


def _mlstm_chunk(cols, write, tril, stril, tril64b, lo_half, convw, convb, gbias, gain, c_ref, n_ref, m_ref,
                 prev_ref):
    wp = MLSTM_WP
    nh = MLSTM_HEADS
    lane_g = lax.broadcasted_iota(jnp.int32, (1, TILE), 1)

    def col(x, h):
        return jnp.sum(jnp.where(lane_g == h, x, 0.0), axis=1, keepdims=True)

    def spread_s(cs):
        return jnp.concatenate([jnp.where(lo_half, cs[2 * p], cs[2 * p + 1]) for p in range(nh // 2)], axis=1)

    def spread_w(cs):
        return jnp.concatenate([jnp.broadcast_to(c, (c.shape[0], TILE)) for c in cs], axis=1)

    off = MLSTM_OFF
    qk_pre = cols(off, off + 2 * wp)
    prev8 = prev_ref[...]
    prev_ref[...] = qk_pre[CHUNK - SUB:CHUNK]
    conv = convb + qk_pre * convw[MLSTM_CONV - 1:MLSTM_CONV, :]
    for j in range(MLSTM_CONV - 1):
        conv = conv + _shift_rows(prev8, qk_pre, MLSTM_CONV - 1 - j) * convw[j:j + 1, :]
    qk = _silu(conv)
    q = qk[:, 0:wp]
    k = qk[:, wp:2 * wp] * (MLSTM_HD ** -0.5)
    v = cols(off + 2 * wp, off + 3 * wp)
    o = cols(off + 3 * wp, off + 4 * wp)
    gates = cols(off + 4 * wp, off + 4 * wp + TILE) + gbias
    i_pre = gates
    log_f = _log_sigmoid(pltpu.roll(gates, TILE - nh, 1))
    bcum = _cumsum_rows(tril64b, log_f)
    yield
    m_prev = m_ref[0:1, :]

    tril2 = jnp.concatenate([tril] * (nh // 2), axis=1)
    eye2 = tril2 - jnp.concatenate([stril] * (nh // 2), axis=1)
    bcum_c = [col(bcum, h) for h in range(nh)]
    src_c = [col(i_pre - bcum, h) for h in range(nh)]
    src_row = jnp.sum(spread_s(src_c) * eye2, axis=0, keepdims=True)
    d_mat = jnp.where(tril2 > 0.0, spread_s(bcum_c) + src_row, -jnp.inf)
    a_c = [bcum_c[h] + col(m_prev, h) for h in range(nh)]
    head_lanes = [lo_half, jnp.logical_not(lo_half)]
    d_tiles = _tiles(d_mat)
    m_c = [jnp.maximum(a_c[h], jnp.max(jnp.where(head_lanes[h % 2], d_tiles[h // 2], -jnp.inf),
                                       axis=1, keepdims=True)) for h in range(nh)]
    inter_c = [jnp.exp(a_c[h] - m_c[h]) for h in range(nh)]
    weights = jnp.exp(d_mat - spread_s(m_c))

    q_t, k_t, v_t = _tiles(q), _tiles(k), _tiles(v)
    zero = jnp.zeros((CHUNK, TILE), BF16)

    def pair_bd(z_t, p):
        a, b = _bf(z_t[2 * p]), _bf(z_t[2 * p + 1])
        return jnp.concatenate([jnp.concatenate([a, zero], axis=1), jnp.concatenate([zero, b], axis=1)], axis=0)

    pairs = range(nh // 2)
    heads = range(nh)
    c_states = [c_ref[h] for h in heads]
    raw = [_mm_nt(jnp.concatenate([q_t[2 * p], q_t[2 * p + 1]], axis=1), pair_bd(k_t, p)) for p in pairs]
    from_state = [_mm(q_t[h], c_states[h]) for h in heads]
    yield
    s_tiles = [raw[p] * weights[:, p * TILE:(p + 1) * TILE] for p in pairs]
    num_tiles = [z for p in pairs for z in _tiles(_mm(s_tiles[p], pair_bd(v_t, p)))]
    n_state = n_ref[0:1, :]
    qn_t = _tiles(q * n_state)
    b_last = bcum[CHUNK - 1:CHUNK, :]
    g_s = b_last - bcum + i_pre
    m_new = jnp.maximum(b_last + m_prev, jnp.max(g_s, axis=0, keepdims=True))
    ws = jnp.exp(g_s - m_new)
    carry = jnp.exp(b_last + m_prev - m_new)
    kw_t = [k_t[h] * col(ws, h) for h in heads]
    for h in heads:
        c_ref[h] = c_states[h] * col(carry, h) + _mm_tn(kw_t[h], v_t[h])
    n_ref[0:1, :] = (n_state * spread_w([col(carry, h) for h in heads])
                     + jnp.sum(jnp.concatenate(kw_t, axis=1), axis=0, keepdims=True))
    m_ref[0:1, :] = m_new
    yield
    pad_mask = jnp.where(lane_g < MLSTM_HD, 1.0, 0.0)
    hs = []
    for h in heads:
        num = num_tiles[h] + inter_c[h] * from_state[h]
        den = (jnp.sum(jnp.where(head_lanes[h % 2], s_tiles[h // 2], 0.0), axis=1, keepdims=True)
               + inter_c[h] * jnp.sum(qn_t[h], axis=1, keepdims=True))
        hc = num / jnp.maximum(jnp.abs(den), jnp.exp(-m_c[h]))
        mean = jnp.sum(hc, axis=1, keepdims=True) * (1.0 / MLSTM_HD)
        yc = (hc - mean) * pad_mask
        var = jnp.sum(yc * yc, axis=1, keepdims=True) * (1.0 / MLSTM_HD)
        hs.append(yc * lax.rsqrt(var + GN_EPS))
    write(jnp.concatenate(hs, axis=1) * gain * jax.nn.sigmoid(o))


def _mixer_kernel(proj_ref, cos_ref, sin_ref, dec_ref, cdec_ref, tri_ref, mask2f_ref, ones2b_ref, tril64b_ref,
                  gret_ref, mu_ref, p384_ref, w2a2_ref, g2_ref, convw_ref, convb_ref, gbias_ref, gml_ref,
                  out_ref,
                  s_ret, s_rwkv, prev_rwkv, c_ml, n_ml, m_ml, prev_qk):
    @pl.when(pl.program_id(1) == 0)
    def _():
        for ref in (s_ret, s_rwkv, prev_rwkv, c_ml, n_ml, m_ml, prev_qk):
            ref[...] = jnp.zeros(ref.shape, ref.dtype)

    tril, stril = tri_ref[0], tri_ref[1]
    mask2f = mask2f_ref[...]
    ones2b = ones2b_ref[...]
    tril64b = tril64b_ref[...]
    lo_half = lax.broadcasted_iota(jnp.int32, (1, TILE), 1) < TILE // 2
    for ci in range(MIX_CHUNKS):
        r0 = ci * CHUNK
        cos = cos_ref[r0:r0 + CHUNK, :]
        sin = sin_ref[r0:r0 + CHUNK, :]
        chains = []
        for bb in range(MIX_BB):
            cols = lambda lo, hi, bb=bb, r0=r0: proj_ref[bb, r0:r0 + CHUNK, lo:hi]

            def writer(lo, hi, bb=bb, r0=r0):
                def write(y):
                    out_ref[bb, r0:r0 + CHUNK, lo:hi] = y.astype(out_ref.dtype)
                return write

            chains.append(_rwkv_chunk(cols, writer(RET_W, RET_W + RWKV_W), tril, stril, tril64b, mask2f, ones2b,
                                      lo_half, mu_ref[...], p384_ref[...], w2a2_ref[...], g2_ref[...],
                                      s_rwkv.at[bb], prev_rwkv.at[bb]))
            chains.append(_retention_chunk(cols, writer(0, RET_W), cos, sin, dec_ref, cdec_ref[...], mask2f,
                                           ones2b, lo_half, gret_ref[...], s_ret.at[bb]))
            chains.append(_mlstm_chunk(cols, writer(RET_W + RWKV_W, D_MIX_PAD), tril, stril, tril64b, lo_half,
                                       convw_ref[...], convb_ref[...], gbias_ref[...], gml_ref[...],
                                       c_ml.at[bb], n_ml.at[bb], m_ml.at[bb], prev_qk.at[bb]))
        while chains:
            chains = [g for g in chains if next(g, _DONE) is not _DONE]


def _mixer_constants(seq):
    h = jnp.arange(RET_HEADS, dtype=F32)
    log_g = jnp.log1p(-jnp.exp2(-5.0 - h))
    idx = jnp.arange(CHUNK, dtype=F32)
    diff = idx[:, None] - idx[None, :]
    intra = jnp.where(diff >= 0, jnp.exp(log_g[:, None, None] * jnp.maximum(diff, 0.0)), 0.0)
    intra = jnp.transpose(intra, (1, 0, 2)).reshape(CHUNK, RET_HEADS * CHUNK)
    q_decay = jnp.exp(log_g[:, None] * (idx + 1.0))
    k_decay = jnp.exp(log_g[:, None] * (CHUNK - 1.0 - idx))
    rep = lambda z: jnp.repeat(z.T, RET_HD, axis=1)
    dec = jnp.stack([intra, rep(q_decay), rep(k_decay)])
    cdec = jnp.repeat(jnp.exp(log_g * CHUNK), RET_HD)[None, :]

    inv = ROPE_BASE ** (-jnp.arange(0, RET_HD, 2, dtype=F32) / RET_HD)
    ang = jnp.arange(seq, dtype=F32)[:, None] * inv[None, :]
    cos = jnp.tile(jnp.cos(ang), (1, 2 * RET_HEADS))
    sin = jnp.tile(jnp.concatenate([-jnp.sin(ang), jnp.sin(ang)], axis=1), (1, RET_HEADS))

    half = np.arange(TILE) // CHUNK
    mask2 = (half[:, None] == half[None, :]).astype(np.float32)
    t = np.arange(CHUNK)
    tril = (t[:, None] >= t[None, :]).astype(np.float32)
    stril = (t[:, None] > t[None, :]).astype(np.float32)
    tri = np.stack([np.tile(tril, (1, 2)), np.tile(stril, (1, 2))])
    return dict(cos=cos, sin=sin, dec=dec, cdec=cdec, tri=jnp.asarray(tri), mask2f=jnp.asarray(mask2),
                ones2b=jnp.asarray(mask2, BF16), tril64b=jnp.asarray(tril, BF16))


def _mixers(proj3, consts, lp):
    batch, seq, _ = proj3.shape
    tt = MIX_CHUNKS * CHUNK
    full = lambda a: pl.BlockSpec(a.shape, lambda b, c: (0,) * a.ndim)
    const_inputs = [consts["dec"], consts["cdec"], consts["tri"], consts["mask2f"], consts["ones2b"],
                    consts["tril64b"]]
    layer_inputs = [lp["gain_ret"], lp["mu"], lp["p384"], lp["w2a2"], lp["g2"], lp["convw"], lp["convb"],
                    lp["gbias"], lp["gain_ml"]]
    return pl.pallas_call(
        _mixer_kernel,
        grid=(batch // MIX_BB, seq // tt),
        in_specs=[pl.BlockSpec((MIX_BB, tt, D_IN_PAD), lambda b, c: (b, c, 0)),
                  pl.BlockSpec((tt, RET_W), lambda b, c: (c, 0)),
                  pl.BlockSpec((tt, RET_W), lambda b, c: (c, 0))]
                 + [full(a) for a in const_inputs + layer_inputs],
        out_specs=pl.BlockSpec((MIX_BB, tt, D_MIX_PAD), lambda b, c: (b, c, 0)),
        out_shape=jax.ShapeDtypeStruct((batch, seq, D_MIX_PAD), BF16),
        scratch_shapes=[
            pltpu.VMEM((MIX_BB, RET_W // TILE, TILE, TILE), F32),
            pltpu.VMEM((MIX_BB, RWKV_W // TILE, TILE, TILE), F32),
            pltpu.VMEM((MIX_BB, SUB, RWKV_IN), F32),
            pltpu.VMEM((MIX_BB, MLSTM_HEADS, TILE, TILE), F32),
            pltpu.VMEM((MIX_BB, SUB, MLSTM_WP), F32),
            pltpu.VMEM((MIX_BB, SUB, TILE), F32),
            pltpu.VMEM((MIX_BB, SUB, 2 * MLSTM_WP), F32),
        ],
        compiler_params=pltpu.CompilerParams(dimension_semantics=("arbitrary", "arbitrary"),
                                             vmem_limit_bytes=VMEM_LIMIT),
        name="mixers",
    )(proj3, consts["cos"], consts["sin"], *const_inputs, *layer_inputs)


def _out_ffn_kernel(x_ref, mix_ref, gm_ref, shf_ref, scf_ref, gf_ref, wout_ref, gain_ref, wi_ref, wo_ref,
                    fin_ref, o_ref, *, final):
    x = x_ref[...] + gm_ref[0] * jnp.dot(mix_ref[...], wout_ref[...], preferred_element_type=F32)
    h = _rms(x) * gain_ref[...]
    h = _bf(h * (1.0 + scf_ref[0]) + shf_ref[0])
    acc = jnp.zeros(x.shape, F32)
    for f0 in range(0, D_FF, FF_CHUNK):
        gate = jnp.dot(h, wi_ref[:, f0:f0 + FF_CHUNK], preferred_element_type=F32)
        up = jnp.dot(h, wi_ref[:, D_FF + f0:D_FF + f0 + FF_CHUNK], preferred_element_type=F32)
        acc = acc + jnp.dot(_bf(_silu(gate) * up), wo_ref[f0:f0 + FF_CHUNK, :], preferred_element_type=F32)
    x = x + gf_ref[0] * acc
    if final:
        x = _rms(x) * fin_ref[...]
    o_ref[...] = x


def _out_ffn(x2, mixed, mod3, w_out_b, gain, wi_b, wo_b, final_gain, seq, final):
    n_tok, d = x2.shape
    tm = 256
    tpb = seq // tm
    full = lambda a: pl.BlockSpec(a.shape, lambda i: (0,) * a.ndim)
    return pl.pallas_call(
        functools.partial(_out_ffn_kernel, final=final),
        grid=(n_tok // tm,),
        in_specs=[
            pl.BlockSpec((tm, d), lambda i: (i, 0)),
            pl.BlockSpec((tm, mixed.shape[1]), lambda i: (i, 0)),
            _mod_spec(2, tpb), _mod_spec(3, tpb), _mod_spec(4, tpb), _mod_spec(5, tpb),
            full(w_out_b), full(gain), full(wi_b), full(wo_b), full(final_gain),
        ],
        out_specs=pl.BlockSpec((tm, d), lambda i: (i, 0)),
        out_shape=jax.ShapeDtypeStruct((n_tok, d), F32),
        compiler_params=pltpu.CompilerParams(dimension_semantics=("arbitrary",),
                                             vmem_limit_bytes=VMEM_LIMIT),
        name="out_ffn",
    )(x2, mixed, mod3, mod3, mod3, mod3, w_out_b, gain, wi_b, wo_b, final_gain)


def _pad_heads_index(n_parts):
    idx = -np.ones(n_parts * MLSTM_WP, np.int64)
    for part in range(n_parts):
        for h in range(MLSTM_HEADS):
            dst = part * MLSTM_WP + h * TILE
            idx[dst:dst + MLSTM_HD] = part * MLSTM_W + h * MLSTM_HD + np.arange(MLSTM_HD)
    return idx


def _gather_cols(a, idx):
    return jnp.where(jnp.asarray(idx >= 0), a[..., np.maximum(idx, 0)], 0.0)


def _layer_params(l, mix_gn, rwkv_mu, rwkv_w0, rwkv_w2, rwkv_a0, rwkv_a2, rwkv_g2, rwkv_k_k,
                  rwkv_k_a, rwkv_r_k, mlstm_conv_w, mlstm_conv_b, mlstm_i_b, mlstm_f_b):
    gn = mix_gn[l]
    zeros = jnp.zeros((RWKV_W,), F32)
    p384 = jnp.stack([rwkv_w0[l], rwkv_a0[l], rwkv_k_k[l], rwkv_k_a[l], rwkv_r_k[l].reshape(RWKV_W),
                      gn[RET_W:RET_W + RWKV_W], zeros, zeros])
    w2a2 = jnp.zeros((LORA_W, 2 * RWKV_W), F32)
    w2a2 = w2a2.at[:RWKV_DECAY_LORA, :RWKV_W].set(rwkv_w2[l]).at[RWKV_DECAY_LORA:, RWKV_W:].set(rwkv_a2[l])
    gbias = jnp.zeros((1, TILE), F32)
    gbias = gbias.at[0, :MLSTM_HEADS].set(mlstm_i_b[l]).at[0, MLSTM_HEADS:2 * MLSTM_HEADS].set(mlstm_f_b[l])
    qk_idx = _pad_heads_index(2)
    convw = jnp.zeros((SUB, 2 * MLSTM_WP), F32).at[:MLSTM_CONV].set(_gather_cols(mlstm_conv_w[l], qk_idx))
    convb = _gather_cols(mlstm_conv_b[l], qk_idx)[None, :]
    gain_ml = _gather_cols(gn[RET_W + RWKV_W:], _pad_heads_index(1))[None, :]
    return dict(gain_ret=gn[None, :RET_W], mu=rwkv_mu[l][None, :], p384=p384, w2a2=_bf(w2a2),
                g2=_bf(rwkv_g2[l]), convw=convw, convb=convb, gbias=gbias, gain_ml=gain_ml)


def _pad_w_in(w):
    idx = _pad_heads_index(4)
    gates = -np.ones(TILE, np.int64)
    gates[:2 * MLSTM_HEADS] = 4 * MLSTM_W + np.arange(2 * MLSTM_HEADS)
    ml = _gather_cols(w[:, MLSTM_OFF:], np.concatenate([idx, gates]))
    return jnp.concatenate([w[:, :MLSTM_OFF], ml], axis=1)


def _pad_w_out(w):
    idx = _pad_heads_index(1)
    ml = _gather_cols(w[RET_W + RWKV_W:].T, idx).T
    return jnp.concatenate([w[:RET_W + RWKV_W], ml], axis=0)


def kernel(x, c, ada_w, ada_b, norm_mix, norm_ffn, w_in, mix_gn, rwkv_mu, rwkv_w0, rwkv_w2, rwkv_a0, rwkv_a2, rwkv_g2, rwkv_k_k, rwkv_k_a, rwkv_r_k, mlstm_conv_w, mlstm_conv_b, mlstm_i_b, mlstm_f_b, w_out, ffn_w_in, ffn_w_out, final_norm):
    batch, seq, d = x.shape
    depth = ada_w.shape[0]
    assert d == D_MODEL and seq % 256 == 0 and batch % MIX_BB == 0
    consts = _mixer_constants(seq)
    mod = _modulation(c, ada_w, ada_b)
    x2 = x.reshape(batch * seq, d)
    final_gain = final_norm[None, :]
    for l in range(depth):
        mod3 = mod[l].reshape(batch, 1, 6 * d)
        lp = _layer_params(l, mix_gn, rwkv_mu, rwkv_w0, rwkv_w2, rwkv_a0, rwkv_a2, rwkv_g2,
                           rwkv_k_k, rwkv_k_a, rwkv_r_k, mlstm_conv_w, mlstm_conv_b, mlstm_i_b, mlstm_f_b)
        proj = _in_projection(x2, mod3, norm_mix[l][None, :], _bf(_pad_w_in(w_in[l])), seq)
        mixed = _mixers(proj.reshape(batch, seq, D_IN_PAD), consts, lp)
        x2 = _out_ffn(x2, mixed.reshape(batch * seq, D_MIX_PAD), mod3, _bf(_pad_w_out(w_out[l])),
                      norm_ffn[l][None, :], _bf(ffn_w_in[l]), _bf(ffn_w_out[l]), final_gain, seq,
                      final=(l == depth - 1))
    return x2.reshape(batch, seq, d)
```

```python
import functools

import numpy as np
import jax
import jax.numpy as jnp
from jax import lax
from jax.experimental import pallas as pl
from jax.experimental.pallas import tpu as pltpu

F32 = jnp.float32
BF16 = jnp.bfloat16

D_MODEL = 1024
CHUNK = 64
RET_HEADS, RET_HD = 4, 64
RET_W = RET_HEADS * RET_HD
RWKV_HEADS, RWKV_HD = 6, 64
RWKV_W = RWKV_HEADS * RWKV_HD
MLSTM_HEADS, MLSTM_HD = 4, 96
MLSTM_W = MLSTM_HEADS * MLSTM_HD
D_MIX = RET_W + RWKV_W + MLSTM_W
RWKV_DECAY_LORA, RWKV_ICL_LORA, RWKV_GATE_LORA = 64, 64, 128
MLSTM_CONV = 4
D_FF = 2816
ROPE_BASE = 10000.0
RMS_EPS = 1e-6
GN_EPS = 1e-5

RET_IN = 4 * RET_W
RWKV_IN = 3 * RWKV_W + RWKV_DECAY_LORA + RWKV_ICL_LORA + RWKV_GATE_LORA
MLSTM_IN = 4 * MLSTM_W + 2 * MLSTM_HEADS
D_IN = RET_IN + RWKV_IN + MLSTM_IN
TILE = 128
SUB = 8
MLSTM_WP = MLSTM_HEADS * TILE
MLSTM_IN_PAD = 4 * MLSTM_WP + TILE
D_IN_PAD = RET_IN + RWKV_IN + MLSTM_IN_PAD
D_MIX_PAD = RET_W + RWKV_W + MLSTM_WP
RWKV_OFF = RET_IN
MLSTM_OFF = RET_IN + RWKV_IN
LORA_W = RWKV_DECAY_LORA + RWKV_ICL_LORA
FF_CHUNK = 256
IN_PIECE = 512
OUT_PIECE = 256
MIX_BB = 4

VMEM_LIMIT = 56 * 1024 * 1024
_DONE = object()


def _bf(x):
    return x.astype(BF16)


def _mm(a, b):
    return jnp.dot(_bf(a), _bf(b), preferred_element_type=F32)


def _mm_nt(a, b):
    return lax.dot_general(_bf(a), _bf(b), (((1,), (1,)), ((), ())), preferred_element_type=F32)


def _mm_tn(a, b):
    return lax.dot_general(_bf(a), _bf(b), (((0,), (0,)), ((), ())), preferred_element_type=F32)


def _split2(x):
    hi = _bf(x)
    lo = _bf(x - hi.astype(F32))
    return hi, lo


def _split3(x):
    hi = _bf(x)
    r = x - hi.astype(F32)
    mid = _bf(r)
    lo = _bf(r - mid.astype(F32))
    return hi, mid, lo


def _tiles(x):
    return [x[:, t * TILE:(t + 1) * TILE] for t in range(x.shape[1] // TILE)]


def _cumsum_rows(tril_b, x):
    w = x.shape[1]
    r = jnp.dot(tril_b, jnp.concatenate(_split3(x), axis=1), preferred_element_type=F32)
    return r[:, 0:w] + r[:, w:2 * w] + r[:, 2 * w:3 * w]


def _segsum64(x, ones2_b):
    rows = x.shape[0]
    parts = [t for p in _split2(x) for t in _tiles(p)]
    r = jnp.dot(jnp.concatenate(parts, axis=0), ones2_b, preferred_element_type=F32)
    nt = len(parts) // 2
    return jnp.concatenate([r[t * rows:(t + 1) * rows] + r[(nt + t) * rows:(nt + t + 1) * rows]
                            for t in range(nt)], axis=1)


def _bd2(y, lo_half):
    yb = _bf(y)
    zero = jnp.zeros_like(yb)
    return jnp.concatenate([jnp.where(lo_half, yb, zero), jnp.where(lo_half, zero, yb)], axis=0)


def _log_sigmoid(x):
    return jnp.minimum(x, 0.0) - jnp.log(1.0 + jnp.exp(-jnp.abs(x)))


def _silu(x):
    return x * jax.nn.sigmoid(x)


def _rms(x):
    return x * lax.rsqrt(jnp.mean(x * x, axis=-1, keepdims=True) + RMS_EPS)


def _shift_rows(prev8, cur, k):
    rolled = pltpu.roll(cur, k, 0)
    row8 = lax.broadcasted_iota(jnp.int32, (SUB, 1), 0)
    first = jnp.where(row8 < k, pltpu.roll(prev8, k, 0), rolled[0:SUB])
    return jnp.concatenate([first, rolled[SUB:]], axis=0)


def _mod_kernel(c_ref, w_ref, b_ref, o_ref):
    cond = _silu(c_ref[...])
    o_ref[0] = _mm(cond, w_ref[0]) + b_ref[0]


def _modulation(c, ada_w, ada_b):
    depth, d, n = ada_w.shape
    b = c.shape[0]
    tn = 1536
    return pl.pallas_call(
        _mod_kernel,
        grid=(depth, n // tn),
        in_specs=[
            pl.BlockSpec((b, d), lambda l, j: (0, 0)),
            pl.BlockSpec((1, d, tn), lambda l, j: (l, 0, j)),
            pl.BlockSpec((1, 1, tn), lambda l, j: (l, 0, j)),
        ],
        out_specs=pl.BlockSpec((1, b, tn), lambda l, j: (l, 0, j)),
        out_shape=jax.ShapeDtypeStruct((depth, b, n), F32),
        compiler_params=pltpu.CompilerParams(dimension_semantics=("arbitrary", "arbitrary"),
                                             vmem_limit_bytes=VMEM_LIMIT),
        name="adaln_modulation",
    )(c, ada_w, ada_b.reshape(depth, 1, n))


def _retention_chunk(cols, write, cos, sin, dec_ref, cdec, mask2f, ones2b, lo_half, gain, state_ref):
    w = RET_W
    lane = lax.broadcasted_iota(jnp.int32, (CHUNK, w), 1)
    first_half = (lane & (RET_HD - 1)) < RET_HD // 2

    def rope(z):
        swapped = jnp.where(first_half, pltpu.roll(z, w - RET_HD // 2, 1), pltpu.roll(z, RET_HD // 2, 1))
        return z * cos + swapped * sin

    q = rope(cols(0, w))
    k = rope(cols(w, 2 * w)) * (RET_HD ** -0.5)
    v = cols(2 * w, 3 * w)
    g = cols(3 * w, 4 * w)
    intra, q_dec = _tiles(dec_ref[0]), _tiles(dec_ref[1])
    q_t, k_t, v_t, kd_t = _tiles(q), _tiles(k), _tiles(v), _tiles(k * dec_ref[2])
    nt = len(q_t)
    states = [state_ref[t] for t in range(nt)]
    scores = [_mm_nt(q_t[t], _bd2(k_t[t], lo_half)) * intra[t] for t in range(nt)]
    inter = [_mm(q_t[t], states[t]) * q_dec[t] for t in range(nt)]
    yield
    y = jnp.concatenate([_mm(scores[t], _bd2(v_t[t], lo_half)) + inter[t] for t in range(nt)], axis=1)
    for t in range(nt):
        state_ref[t] = states[t] * cdec[:, t * TILE:(t + 1) * TILE] + mask2f * _mm_tn(kd_t[t], v_t[t])
    yield
    mu = _segsum64(y, ones2b) * (1.0 / RET_HD)
    yield
    yc = y - mu
    var = _segsum64(yc * yc, ones2b) * (1.0 / RET_HD)
    write(yc * lax.rsqrt(var + GN_EPS) * gain * _silu(g))


def _rwkv_chunk(cols, write, tril, stril, tril64b, mask2f, ones2b, lo_half, mu, p384, w2a2, g2, state_ref,
                prev_ref):
    w = RWKV_W
    x = cols(RWKV_OFF, RWKV_OFF + RWKV_IN)
    prev = _shift_rows(prev_ref[...], x, 1)
    prev_ref[...] = x[CHUNK - SUB:CHUNK]
    xs = x + (prev - x) * mu
    r = xs[:, 0:w]
    k = xs[:, w:2 * w]
    v = xs[:, 2 * w:3 * w]
    lora = xs[:, 3 * w:3 * w + LORA_W]
    gl = xs[:, 3 * w + LORA_W:RWKV_IN]
    w0, a0, k_k, k_a, r_k, gain = (p384[i:i + 1, :] for i in range(6))

    lane = lax.broadcasted_iota(jnp.int32, (CHUNK, LORA_W), 1)
    wa = _mm(jnp.where(lane < RWKV_DECAY_LORA, jnp.tanh(lora), lora), w2a2)
    g = _mm(jax.nn.sigmoid(gl), g2)
    kk = k * k_k
    kk_sq = _segsum64(kk * kk, ones2b)
    yield
    log_decay = -jnp.exp(_log_sigmoid(w0 + wa[:, 0:w]) - 0.5)
    a = jax.nn.sigmoid(a0 + wa[:, w:2 * w])
    kk = kk / jnp.maximum(jnp.sqrt(kk_sq), 1e-12)
    k = k * (1.0 + (a - 1.0) * k_a)
    bonus = _segsum64(r * k * r_k, ones2b) * v
    c = _cumsum_rows(tril64b, log_decay)
    yield
    c_last = c[CHUNK - 1:CHUNK, :]
    p_inv = jnp.exp(-c)
    p_end = jnp.exp(c_last - c)
    p_last = jnp.exp(c_last)
    rh = r * jnp.exp(c)
    bh = -kk * jnp.exp(c - log_decay)
    ka = kk * a
    lhs_t = _tiles(jnp.concatenate([bh, rh], axis=0))
    upd_t = _tiles(jnp.concatenate([ka * p_end, k * p_end], axis=0))
    ah_t, kh_t, v_t = _tiles(ka * p_inv), _tiles(k * p_inv), _tiles(v)
    nt = len(v_t)
    tiles = range(nt)

    states = [state_ref[t] for t in tiles]
    s_a = [_mm_nt(lhs_t[t], _bd2(ah_t[t], lo_half)) for t in tiles]
    s_k = [_mm_nt(lhs_t[t], _bd2(kh_t[t], lo_half)) for t in tiles]
    from_state = [_mm_nt(lhs_t[t], states[t]) for t in tiles]
    bd_v = [_bd2(v_t[t], lo_half) for t in tiles]
    yield
    u = [from_state[t][0:CHUNK] + _mm(s_k[t][0:CHUNK] * stril, bd_v[t]) for t in tiles]
    n_pow = [s_a[t][0:CHUNK] * stril for t in tiles]
    y_part = [from_state[t][CHUNK:] + _mm(s_k[t][CHUNK:] * tril, bd_v[t]) for t in tiles]
    yield
    steps = CHUNK.bit_length() - 1
    for j in range(steps):
        u = [u[t] + _mm(n_pow[t], _bd2(u[t], lo_half)) for t in tiles]
        if j + 1 < steps:
            n_pow = [_mm(n_pow[t], _bd2(n_pow[t], lo_half)) for t in tiles]
        yield
    y = jnp.concatenate([y_part[t] + _mm(s_a[t][CHUNK:] * tril, _bd2(u[t], lo_half)) for t in tiles], axis=1)
    for t in tiles:
        upd = _mm_tn(jnp.concatenate([u[t], v_t[t]], axis=0), upd_t[t])
        state_ref[t] = states[t] * p_last[:, t * TILE:(t + 1) * TILE] + mask2f * upd
    yield
    mean = _segsum64(y, ones2b) * (1.0 / RWKV_HD)
    yield
    yc = y - mean
    var = _segsum64(yc * yc, ones2b) * (1.0 / RWKV_HD)
    write((yc * lax.rsqrt(var + GN_EPS) * gain + bonus) * g)


def _mlstm_chunk(cols, write, tril, stril, tril64b, lo_half, convw, convb, gbias, gain, c_ref, n_ref, m_ref,
                 prev_ref):
    wp = MLSTM_WP
    nh = MLSTM_HEADS
    lane_g = lax.broadcasted_iota(jnp.int32, (1, TILE), 1)

    def col(x, h):
        return jnp.sum(jnp.where(lane_g == h, x, 0.0), axis=1, keepdims=True)

    def spread_s(cs):
        return jnp.concatenate([jnp.where(lo_half, cs[2 * p], cs[2 * p + 1]) for p in range(nh // 2)], axis=1)

    def spread_w(cs):
        return jnp.concatenate([jnp.broadcast_to(c, (c.shape[0], TILE)) for c in cs], axis=1)

    off = MLSTM_OFF
    qk_pre = cols(off, off + 2 * wp)
    prev8 = prev_ref[...]
    prev_ref[...] = qk_pre[CHUNK - SUB:CHUNK]
    conv = convb + qk_pre * convw[MLSTM_CONV - 1:MLSTM_CONV, :]
    for j in range(MLSTM_CONV - 1):
        conv = conv + _shift_rows(prev8, qk_pre, MLSTM_CONV - 1 - j) * convw[j:j + 1, :]
    qk = _silu(conv)
    q = qk[:, 0:wp]
    k = qk[:, wp:2 * wp] * (MLSTM_HD ** -0.5)
    v = cols(off + 2 * wp, off + 3 * wp)
    o = cols(off + 3 * wp, off + 4 * wp)
    gates = cols(off + 4 * wp, off + 4 * wp + TILE) + gbias
    i_pre = gates
    log_f = _log_sigmoid(pltpu.roll(gates, TILE - nh, 1))
    bcum = _cumsum_rows(tril64b, log_f)
    yield
    m_prev = m_ref[0:1, :]

    tril2 = jnp.concatenate([tril] * (nh // 2), axis=1)
    eye2 = tril2 - jnp.concatenate([stril] * (nh // 2), axis=1)
    bcum_c = [col(bcum, h) for h in range(nh)]
    src_c = [col(i_pre - bcum, h) for h in range(nh)]
    src_row = jnp.sum(spread_s(src_c) * eye2, axis=0, keepdims=True)
    d_mat = jnp.where(tril2 > 0.0, spread_s(bcum_c) + src_row, -jnp.inf)
    a_c = [bcum_c[h] + col(m_prev, h) for h in range(nh)]
    head_lanes = [lo_half, jnp.logical_not(lo_half)]
    d_tiles = _tiles(d_mat)
    m_c = [jnp.maximum(a_c[h], jnp.max(jnp.where(head_lanes[h % 2], d_tiles[h // 2], -jnp.inf),
                                       axis=1, keepdims=True)) for h in range(nh)]
    inter_c = [jnp.exp(a_c[h] - m_c[h]) for h in range(nh)]
    weights = jnp.exp(d_mat - spread_s(m_c))

    q_t, k_t, v_t = _tiles(q), _tiles(k), _tiles(v)
    zero = jnp.zeros((CHUNK, TILE), BF16)

    def pair_bd(z_t, p):
        a, b = _bf(z_t[2 * p]), _bf(z_t[2 * p + 1])
        return jnp.concatenate([jnp.concatenate([a, zero], axis=1), jnp.concatenate([zero, b], axis=1)], axis=0)

    pairs = range(nh // 2)
    heads = range(nh)
    c_states = [c_ref[h] for h in heads]
    raw = [_mm_nt(jnp.concatenate([q_t[2 * p], q_t[2 * p + 1]], axis=1), pair_bd(k_t, p)) for p in pairs]
    from_state = [_mm(q_t[h], c_states[h]) for h in heads]
    yield
    s_tiles = [raw[p] * weights[:, p * TILE:(p + 1) * TILE] for p in pairs]
    num_tiles = [z for p in pairs for z in _tiles(_mm(s_tiles[p], pair_bd(v_t, p)))]
    n_state = n_ref[0:1, :]
    qn_t = _tiles(q * n_state)
    b_last = bcum[CHUNK - 1:CHUNK, :]
    g_s = b_last - bcum + i_pre
    m_new = jnp.maximum(b_last + m_prev, jnp.max(g_s, axis=0, keepdims=True))
    ws = jnp.exp(g_s - m_new)
    carry = jnp.exp(b_last + m_prev - m_new)
    kw_t = [k_t[h] * col(ws, h) for h in heads]
    for h in heads:
        c_ref[h] = c_states[h] * col(carry, h) + _mm_tn(kw_t[h], v_t[h])
    n_ref[0:1, :] = (n_state * spread_w([col(carry, h) for h in heads])
                     + jnp.sum(jnp.concatenate(kw_t, axis=1), axis=0, keepdims=True))
    m_ref[0:1, :] = m_new
    yield
    pad_mask = jnp.where(lane_g < MLSTM_HD, 1.0, 0.0)
    hs = []
    for h in heads:
        num = num_tiles[h] + inter_c[h] * from_state[h]
        den = (jnp.sum(jnp.where(head_lanes[h % 2], s_tiles[h // 2], 0.0), axis=1, keepdims=True)
               + inter_c[h] * jnp.sum(qn_t[h], axis=1, keepdims=True))
        hc = num / jnp.maximum(jnp.abs(den), jnp.exp(-m_c[h]))
        mean = jnp.sum(hc, axis=1, keepdims=True) * (1.0 / MLSTM_HD)
        yc = (hc - mean) * pad_mask
        var = jnp.sum(yc * yc, axis=1, keepdims=True) * (1.0 / MLSTM_HD)
        hs.append(yc * lax.rsqrt(var + GN_EPS))
    write(jnp.concatenate(hs, axis=1) * gain * jax.nn.sigmoid(o))


def _mixer_constants(seq):
    h = jnp.arange(RET_HEADS, dtype=F32)
    log_g = jnp.log1p(-jnp.exp2(-5.0 - h))
    idx = jnp.arange(CHUNK, dtype=F32)
    diff = idx[:, None] - idx[None, :]
    intra = jnp.where(diff >= 0, jnp.exp(log_g[:, None, None] * jnp.maximum(diff, 0.0)), 0.0)
    intra = jnp.transpose(intra, (1, 0, 2)).reshape(CHUNK, RET_HEADS * CHUNK)
    q_decay = jnp.exp(log_g[:, None] * (idx + 1.0))
    k_decay = jnp.exp(log_g[:, None] * (CHUNK - 1.0 - idx))
    rep = lambda z: jnp.repeat(z.T, RET_HD, axis=1)
    dec = jnp.stack([intra, rep(q_decay), rep(k_decay)])
    cdec = jnp.repeat(jnp.exp(log_g * CHUNK), RET_HD)[None, :]

    inv = ROPE_BASE ** (-jnp.arange(0, RET_HD, 2, dtype=F32) / RET_HD)
    ang = jnp.arange(seq, dtype=F32)[:, None] * inv[None, :]
    cos = jnp.tile(jnp.cos(ang), (1, 2 * RET_HEADS))
    sin = jnp.tile(jnp.concatenate([-jnp.sin(ang), jnp.sin(ang)], axis=1), (1, RET_HEADS))

    half = np.arange(TILE) // CHUNK
    mask2 = (half[:, None] == half[None, :]).astype(np.float32)
    t = np.arange(CHUNK)
    tril = (t[:, None] >= t[None, :]).astype(np.float32)
    stril = (t[:, None] > t[None, :]).astype(np.float32)
    tri = np.stack([np.tile(tril, (1, 2)), np.tile(stril, (1, 2))])
    return dict(cos=cos, sin=sin, dec=dec, cdec=cdec, tri=jnp.asarray(tri), mask2f=jnp.asarray(mask2),
                ones2b=jnp.asarray(mask2, BF16), tril64b=jnp.asarray(tril, BF16))


def _dense_chain(xin_ref, xres_ref, sh_ref, sc_ref, gm_ref, shf_ref, scf_ref, gf_ref, gin_ref, win_ref,
                 wout_ref, gff_ref, wi_ref, wo_ref, fin_ref, out_ref, proj_slot, mixed_slot, act_scr, x1_scr,
                 final):
    rows = range(MIX_BB)
    h = jnp.concatenate([_bf(_rms(xin_ref[b]) * gin_ref[...] * (1.0 + sc_ref[b]) + sh_ref[b]) for b in rows],
                        axis=0)
    for n0 in range(0, D_IN_PAD, IN_PIECE):
        proj_slot[:, n0:n0 + IN_PIECE] = jnp.dot(h, win_ref[:, n0:n0 + IN_PIECE], preferred_element_type=F32)
        yield
    att = jnp.dot(mixed_slot[...], wout_ref[...], preferred_element_type=F32)
    yield
    hs = []
    for b in rows:
        x1 = xres_ref[b] + gm_ref[b] * att[b * CHUNK:(b + 1) * CHUNK]
        x1_scr[b * CHUNK:(b + 1) * CHUNK, :] = x1
        hs.append(_bf(_rms(x1) * gff_ref[...] * (1.0 + scf_ref[b]) + shf_ref[b]))
    h = jnp.concatenate(hs, axis=0)
    for f0 in range(0, D_FF, FF_CHUNK):
        gate = jnp.dot(h, wi_ref[:, f0:f0 + FF_CHUNK], preferred_element_type=F32)
        up = jnp.dot(h, wi_ref[:, D_FF + f0:D_FF + f0 + FF_CHUNK], preferred_element_type=F32)
        act_scr[:, f0:f0 + FF_CHUNK] = _bf(_silu(gate) * up)
        yield
    pieces = []
    for n0 in range(0, D_MODEL, OUT_PIECE):
        pieces.append(jnp.dot(act_scr[...], wo_ref[:, n0:n0 + OUT_PIECE], preferred_element_type=F32))
        yield
    ffn = jnp.concatenate(pieces, axis=1)
    for b in rows:
        x2 = x1_scr[b * CHUNK:(b + 1) * CHUNK, :] + gf_ref[b] * ffn[b * CHUNK:(b + 1) * CHUNK]
        if final:
            x2 = _rms(x2) * fin_ref[...]
        out_ref[b] = x2


def _layer_kernel(xin_ref, xres_ref, sh_ref, sc_ref, gm_ref, shf_ref, scf_ref, gf_ref, cos_ref, sin_ref,
                  dec_ref, cdec_ref, tri_ref, mask2f_ref, ones2b_ref, tril64b_ref,
                  gret_ref, mu_ref, p384_ref, w2a2_ref, g2_ref, convw_ref, convb_ref, gbias_ref, gml_ref,
                  gin_ref, win_ref, wout_ref, gff_ref, wi_ref, wo_ref, fin_ref,
                  out_ref,
                  proj_scr, mixed_scr, act_scr, x1_scr, s_ret, s_rwkv, prev_rwkv, c_ml, n_ml, m_ml, prev_qk,
                  *, n_chunks, n_tiles, final):
    s = pl.program_id(0)
    mix_tile = jnp.clip(s - 1, 0, n_tiles - 1)
    slot_in = lax.rem(s, 2)
    slot_mix = 1 - slot_in

    @pl.when(s == 0)
    def _():
        proj_scr[1] = jnp.zeros(proj_scr.shape[1:], proj_scr.dtype)
        mixed_scr[0] = jnp.zeros(mixed_scr.shape[1:], mixed_scr.dtype)

    @pl.when(lax.rem(mix_tile, n_chunks) == 0)
    def _():
        for ref in (s_ret, s_rwkv, prev_rwkv, c_ml, n_ml, m_ml, prev_qk):
            ref[...] = jnp.zeros(ref.shape, ref.dtype)

    tril, stril = tri_ref[0], tri_ref[1]
    mask2f = mask2f_ref[...]
    ones2b = ones2b_ref[...]
    tril64b = tril64b_ref[...]
    lo_half = lax.broadcasted_iota(jnp.int32, (1, TILE), 1) < TILE // 2
    cos = cos_ref[...]
    sin = sin_ref[...]
    proj_mix = proj_scr.at[slot_mix]
    mixed_mix = mixed_scr.at[slot_mix]
    chains = [_dense_chain(xin_ref, xres_ref, sh_ref, sc_ref, gm_ref, shf_ref, scf_ref, gf_ref, gin_ref, win_ref,
                           wout_ref, gff_ref, wi_ref, wo_ref, fin_ref, out_ref, proj_scr.at[slot_in],
                           mixed_scr.at[slot_in], act_scr, x1_scr, final)]
    for bb in range(MIX_BB):
        r0 = bb * CHUNK
        cols = lambda lo, hi, r0=r0: proj_mix[r0:r0 + CHUNK, lo:hi]

        def writer(lo, hi, r0=r0):
            def write(y):
                mixed_mix[r0:r0 + CHUNK, lo:hi] = y.astype(mixed_mix.dtype)
            return write

        chains.append(_rwkv_chunk(cols, writer(RET_W, RET_W + RWKV_W), tril, stril, tril64b, mask2f, ones2b,
                                  lo_half, mu_ref[...], p384_ref[...], w2a2_ref[...], g2_ref[...],
                                  s_rwkv.at[bb], prev_rwkv.at[bb]))
        chains.append(_retention_chunk(cols, writer(0, RET_W), cos, sin, dec_ref, cdec_ref[...], mask2f,
                                       ones2b, lo_half, gret_ref[...], s_ret.at[bb]))
        chains.append(_mlstm_chunk(cols, writer(RET_W + RWKV_W, D_MIX_PAD), tril, stril, tril64b, lo_half,
                                   convw_ref[...], convb_ref[...], gbias_ref[...], gml_ref[...],
                                   c_ml.at[bb], n_ml.at[bb], m_ml.at[bb], prev_qk.at[bb]))
    while chains:
        chains = [g for g in chains if next(g, _DONE) is not _DONE]


def _layer(x3, mod3, consts, lp, dense, final):
    batch, seq, d = x3.shape
    n_chunks = seq // CHUNK
    n_tiles = (batch // MIX_BB) * n_chunks
    rows = MIX_BB * CHUNK

    def tile(lag):
        def index(s):
            t = jnp.clip(s - lag, 0, n_tiles - 1)
            return t // n_chunks, lax.rem(t, n_chunks)
        return index

    x_spec = lambda lag: pl.BlockSpec((MIX_BB, CHUNK, d), lambda s: (*tile(lag)(s), 0))
    mod_spec = lambda lag, col: pl.BlockSpec((MIX_BB, 1, d), lambda s: (tile(lag)(s)[0], 0, col))
    rope_spec = pl.BlockSpec((CHUNK, RET_W), lambda s: (tile(1)(s)[1], 0))
    full = lambda a: pl.BlockSpec(a.shape, lambda s: (0,) * a.ndim, pipeline_mode=pl.Buffered(1))
    const_inputs = [consts["dec"], consts["cdec"], consts["tri"], consts["mask2f"], consts["ones2b"],
                    consts["tril64b"]]
    layer_inputs = [lp["gain_ret"], lp["mu"], lp["p384"], lp["w2a2"], lp["g2"], lp["convw"], lp["convb"],
                    lp["gbias"], lp["gain_ml"]]
    dense_inputs = [dense["gain_in"], dense["w_in"], dense["w_out"], dense["gain_ffn"], dense["wi"], dense["wo"],
                    dense["final_gain"]]
    return pl.pallas_call(
        functools.partial(_layer_kernel, n_chunks=n_chunks, n_tiles=n_tiles, final=final),
        grid=(n_tiles + 2,),
        in_specs=[x_spec(0), x_spec(2), mod_spec(0, 0), mod_spec(0, 1), mod_spec(2, 2), mod_spec(2, 3),
                  mod_spec(2, 4), mod_spec(2, 5), rope_spec, rope_spec]
                 + [full(a) for a in const_inputs + layer_inputs + dense_inputs],
        out_specs=x_spec(2),
        out_shape=jax.ShapeDtypeStruct((batch, seq, d), F32),
        scratch_shapes=[
            pltpu.VMEM((2, rows, D_IN_PAD), F32),
            pltpu.VMEM((2, rows, D_MIX_PAD), BF16),
            pltpu.VMEM((rows, D_FF), BF16),
            pltpu.VMEM((rows, d), F32),
            pltpu.VMEM((MIX_BB, RET_W // TILE, TILE, TILE), F32),
            pltpu.VMEM((MIX_BB, RWKV_W // TILE, TILE, TILE), F32),
            pltpu.VMEM((MIX_BB, SUB, RWKV_IN), F32),
            pltpu.VMEM((MIX_BB, MLSTM_HEADS, TILE, TILE), F32),
            pltpu.VMEM((MIX_BB, SUB, MLSTM_WP), F32),
            pltpu.VMEM((MIX_BB, SUB, TILE), F32),
            pltpu.VMEM((MIX_BB, SUB, 2 * MLSTM_WP), F32),
        ],
        compiler_params=pltpu.CompilerParams(dimension_semantics=("arbitrary",),
                                             vmem_limit_bytes=VMEM_LIMIT),
        name="layer",
    )(x3, x3, mod3, mod3, mod3, mod3, mod3, mod3, consts["cos"], consts["sin"],
      *const_inputs, *layer_inputs, *dense_inputs)


def _pad_heads_index(n_parts):
    idx = -np.ones(n_parts * MLSTM_WP, np.int64)
    for part in range(n_parts):
        for h in range(MLSTM_HEADS):
            dst = part * MLSTM_WP + h * TILE
            idx[dst:dst + MLSTM_HD] = part * MLSTM_W + h * MLSTM_HD + np.arange(MLSTM_HD)
    return idx


def _gather_cols(a, idx):
    return jnp.where(jnp.asarray(idx >= 0), a[..., np.maximum(idx, 0)], 0.0)


def _layer_params(l, mix_gn, rwkv_mu, rwkv_w0, rwkv_w2, rwkv_a0, rwkv_a2, rwkv_g2, rwkv_k_k,
                  rwkv_k_a, rwkv_r_k, mlstm_conv_w, mlstm_conv_b, mlstm_i_b, mlstm_f_b):
    gn = mix_gn[l]
    zeros = jnp.zeros((RWKV_W,), F32)
    p384 = jnp.stack([rwkv_w0[l], rwkv_a0[l], rwkv_k_k[l], rwkv_k_a[l], rwkv_r_k[l].reshape(RWKV_W),
                      gn[RET_W:RET_W + RWKV_W], zeros, zeros])
    w2a2 = jnp.zeros((LORA_W, 2 * RWKV_W), F32)
    w2a2 = w2a2.at[:RWKV_DECAY_LORA, :RWKV_W].set(rwkv_w2[l]).at[RWKV_DECAY_LORA:, RWKV_W:].set(rwkv_a2[l])
    gbias = jnp.zeros((1, TILE), F32)
    gbias = gbias.at[0, :MLSTM_HEADS].set(mlstm_i_b[l]).at[0, MLSTM_HEADS:2 * MLSTM_HEADS].set(mlstm_f_b[l])
    qk_idx = _pad_heads_index(2)
    convw = jnp.zeros((SUB, 2 * MLSTM_WP), F32).at[:MLSTM_CONV].set(_gather_cols(mlstm_conv_w[l], qk_idx))
    convb = _gather_cols(mlstm_conv_b[l], qk_idx)[None, :]
    gain_ml = _gather_cols(gn[RET_W + RWKV_W:], _pad_heads_index(1))[None, :]
    return dict(gain_ret=gn[None, :RET_W], mu=rwkv_mu[l][None, :], p384=p384, w2a2=_bf(w2a2),
                g2=_bf(rwkv_g2[l]), convw=convw, convb=convb, gbias=gbias, gain_ml=gain_ml)


def _pad_w_in(w):
    idx = _pad_heads_index(4)
    gates = -np.ones(TILE, np.int64)
    gates[:2 * MLSTM_HEADS] = 4 * MLSTM_W + np.arange(2 * MLSTM_HEADS)
    ml = _gather_cols(w[:, MLSTM_OFF:], np.concatenate([idx, gates]))
    return jnp.concatenate([w[:, :MLSTM_OFF], ml], axis=1)


def _pad_w_out(w):
    idx = _pad_heads_index(1)
    ml = _gather_cols(w[RET_W + RWKV_W:].T, idx).T
    return jnp.concatenate([w[:RET_W + RWKV_W], ml], axis=0)


def kernel(x, c, ada_w, ada_b, norm_mix, norm_ffn, w_in, mix_gn, rwkv_mu, rwkv_w0, rwkv_w2, rwkv_a0, rwkv_a2, rwkv_g2, rwkv_k_k, rwkv_k_a, rwkv_r_k, mlstm_conv_w, mlstm_conv_b, mlstm_i_b, mlstm_f_b, w_out, ffn_w_in, ffn_w_out, final_norm):
    batch, seq, d = x.shape
    depth = ada_w.shape[0]
    assert d == D_MODEL and seq % 256 == 0 and batch % MIX_BB == 0
    consts = _mixer_constants(seq)
    mod = _modulation(c, ada_w, ada_b)
    for l in range(depth):
        mod3 = mod[l].reshape(batch, 1, 6 * d)
        lp = _layer_params(l, mix_gn, rwkv_mu, rwkv_w0, rwkv_w2, rwkv_a0, rwkv_a2, rwkv_g2,
                           rwkv_k_k, rwkv_k_a, rwkv_r_k, mlstm_conv_w, mlstm_conv_b, mlstm_i_b, mlstm_f_b)
        dense = dict(gain_in=norm_mix[l][None, :], w_in=_bf(_pad_w_in(w_in[l])), w_out=_bf(_pad_w_out(w_out[l])),
                     gain_ffn=norm_ffn[l][None, :], wi=_bf(ffn_w_in[l]), wo=_bf(ffn_w_out[l]),
                     final_gain=final_norm[None, :])
        x = _layer(x, mod3, consts, lp, dense, final=(l == depth - 1))
    return x
```

```python
import functools

import numpy as np
import jax
import jax.numpy as jnp
from jax import lax
from jax.experimental import pallas as pl
from jax.experimental.pallas import tpu as pltpu

F32 = jnp.float32
BF16 = jnp.bfloat16

D_MODEL = 1024
CHUNK = 64
RET_HEADS, RET_HD = 4, 64
RET_W = RET_HEADS * RET_HD
RWKV_HEADS, RWKV_HD = 6, 64
RWKV_W = RWKV_HEADS * RWKV_HD
MLSTM_HEADS, MLSTM_HD = 4, 96
MLSTM_W = MLSTM_HEADS * MLSTM_HD
D_MIX = RET_W + RWKV_W + MLSTM_W
RWKV_DECAY_LORA, RWKV_ICL_LORA, RWKV_GATE_LORA = 64, 64, 128
MLSTM_CONV = 4
D_FF = 2816
ROPE_BASE = 10000.0
RMS_EPS = 1e-6
GN_EPS = 1e-5

RET_IN = 4 * RET_W
RWKV_IN = 3 * RWKV_W + RWKV_DECAY_LORA + RWKV_ICL_LORA + RWKV_GATE_LORA
MLSTM_IN = 4 * MLSTM_W + 2 * MLSTM_HEADS
D_IN = RET_IN + RWKV_IN + MLSTM_IN
TILE = 128
SUB = 8
MLSTM_WP = MLSTM_HEADS * TILE
MLSTM_IN_PAD = 4 * MLSTM_WP + TILE
D_IN_PAD = RET_IN + RWKV_IN + MLSTM_IN_PAD
D_MIX_PAD = RET_W + RWKV_W + MLSTM_WP
RWKV_OFF = RET_IN
MLSTM_OFF = RET_IN + RWKV_IN
LORA_W = RWKV_DECAY_LORA + RWKV_ICL_LORA
FF_CHUNK = 256
IN_PIECE = 512
OUT_PIECE = 256
MIX_BB = 4

WAVE_DELAY = 7
VMEM_LIMIT = 60000 * 1024
_DONE = object()


def _bf(x):
    return x.astype(BF16)


def _mm(a, b):
    return jnp.dot(_bf(a), _bf(b), preferred_element_type=F32)


def _mm_nt(a, b):
    return lax.dot_general(_bf(a), _bf(b), (((1,), (1,)), ((), ())), preferred_element_type=F32)


def _mm_tn(a, b):
    return lax.dot_general(_bf(a), _bf(b), (((0,), (0,)), ((), ())), preferred_element_type=F32)


def _split3(x):
    hi = _bf(x)
    r = x - hi.astype(F32)
    mid = _bf(r)
    lo = _bf(r - mid.astype(F32))
    return hi, mid, lo


def _tiles(x):
    return [x[:, t * TILE:(t + 1) * TILE] for t in range(x.shape[1] // TILE)]


def _cumsum_rows(tril_b, x):
    w = x.shape[1]
    r = jnp.dot(tril_b, jnp.concatenate(_split3(x), axis=1), preferred_element_type=F32)
    return r[:, 0:w] + r[:, w:2 * w] + r[:, 2 * w:3 * w]


def _segsum64(x, lo_half):
    outs = []
    for t in _tiles(x):
        lo = jnp.sum(jnp.where(lo_half, t, 0.0), axis=1, keepdims=True)
        hi = jnp.sum(jnp.where(lo_half, 0.0, t), axis=1, keepdims=True)
        outs.append(jnp.where(lo_half, lo, hi))
    return jnp.concatenate(outs, axis=1)


def _bd2(y, lo_half):
    yb = _bf(y)
    zero = jnp.zeros_like(yb)
    return jnp.concatenate([jnp.where(lo_half, yb, zero), jnp.where(lo_half, zero, yb)], axis=0)


def _log_sigmoid(x):
    return jnp.minimum(x, 0.0) - jnp.log(1.0 + jnp.exp(-jnp.abs(x)))


def _silu(x):
    return x * jax.nn.sigmoid(x)


def _rms(x):
    return x * lax.rsqrt(jnp.mean(x * x, axis=-1, keepdims=True) + RMS_EPS)


def _shift_rows(prev8, cur, k):
    rolled = pltpu.roll(cur, k, 0)
    row8 = lax.broadcasted_iota(jnp.int32, (SUB, 1), 0)
    first = jnp.where(row8 < k, pltpu.roll(prev8, k, 0), rolled[0:SUB])
    return jnp.concatenate([first, rolled[SUB:]], axis=0)


def _mod_kernel(c_ref, w_ref, b_ref, o_ref):
    cond = _silu(c_ref[...])
    o_ref[0] = _mm(cond, w_ref[0]) + b_ref[0]


def _modulation(c, ada_w, ada_b):
    depth, d, n = ada_w.shape
    b = c.shape[0]
    tn = 1536
    return pl.pallas_call(
        _mod_kernel,
        grid=(depth, n // tn),
        in_specs=[
            pl.BlockSpec((b, d), lambda l, j: (0, 0)),
            pl.BlockSpec((1, d, tn), lambda l, j: (l, 0, j)),
            pl.BlockSpec((1, 1, tn), lambda l, j: (l, 0, j)),
        ],
        out_specs=pl.BlockSpec((1, b, tn), lambda l, j: (l, 0, j)),
        out_shape=jax.ShapeDtypeStruct((depth, b, n), F32),
        compiler_params=pltpu.CompilerParams(dimension_semantics=("arbitrary", "arbitrary"),
                                             vmem_limit_bytes=VMEM_LIMIT),
        name="adaln_modulation",
    )(c, ada_w, ada_b.reshape(depth, 1, n))


def _retention_chunk(cols, write, cos, sin, dec_ref, cdec, mask2f, lo_half, gain, state_ref):
    w = RET_W
    lane = lax.broadcasted_iota(jnp.int32, (CHUNK, w), 1)
    first_half = (lane & (RET_HD - 1)) < RET_HD // 2

    def rope(z):
        swapped = jnp.where(first_half, pltpu.roll(z, w - RET_HD // 2, 1), pltpu.roll(z, RET_HD // 2, 1))
        return z * cos + swapped * sin

    q = rope(cols(0, w))
    k = rope(cols(w, 2 * w)) * (RET_HD ** -0.5)
    v = cols(2 * w, 3 * w)
    g = cols(3 * w, 4 * w)
    intra, q_dec = _tiles(dec_ref[0]), _tiles(dec_ref[1])
    q_t, k_t, v_t, kd_t = _tiles(q), _tiles(k), _tiles(v), _tiles(k * dec_ref[2])
    nt = len(q_t)
    states = [state_ref[t] for t in range(nt)]
    scores = [_mm_nt(q_t[t], _bd2(k_t[t], lo_half)) * intra[t] for t in range(nt)]
    inter = [_mm(q_t[t], states[t]) * q_dec[t] for t in range(nt)]
    yield
    y = jnp.concatenate([_mm(scores[t], _bd2(v_t[t], lo_half)) + inter[t] for t in range(nt)], axis=1)
    for t in range(nt):
        state_ref[t] = states[t] * cdec[:, t * TILE:(t + 1) * TILE] + mask2f * _mm_tn(kd_t[t], v_t[t])
    yield
    yc = y - _segsum64(y, lo_half) * (1.0 / RET_HD)
    var = _segsum64(yc * yc, lo_half) * (1.0 / RET_HD)
    write(yc * lax.rsqrt(var + GN_EPS) * gain * _silu(g))


def _rwkv_chunk(cols, write, tril, stril, tril64b, mask2f, lo_half, mu, p384, w2a2, g2, state_ref, prev_ref):
    w = RWKV_W
    x = cols(RWKV_OFF, RWKV_OFF + RWKV_IN)
    prev = _shift_rows(prev_ref[...], x, 1)
    prev_ref[...] = x[CHUNK - SUB:CHUNK]
    xs = x + (prev - x) * mu
    r = xs[:, 0:w]
    k = xs[:, w:2 * w]
    v = xs[:, 2 * w:3 * w]
    lora = xs[:, 3 * w:3 * w + LORA_W]
    gl = xs[:, 3 * w + LORA_W:RWKV_IN]
    w0, a0, k_k, k_a, r_k, gain = (p384[i:i + 1, :] for i in range(6))

    lane = lax.broadcasted_iota(jnp.int32, (CHUNK, LORA_W), 1)
    wa = _mm(jnp.where(lane < RWKV_DECAY_LORA, jnp.tanh(lora), lora), w2a2)
    g = _mm(jax.nn.sigmoid(gl), g2)
    kk = k * k_k
    kk_sq = _segsum64(kk * kk, lo_half)
    yield
    log_decay = -jnp.exp(_log_sigmoid(w0 + wa[:, 0:w]) - 0.5)
    a = jax.nn.sigmoid(a0 + wa[:, w:2 * w])
    kk = kk / jnp.maximum(jnp.sqrt(kk_sq), 1e-12)
    k = k * (1.0 + (a - 1.0) * k_a)
    bonus = _segsum64(r * k * r_k, lo_half) * v
    c = _cumsum_rows(tril64b, log_decay)
    yield
    c_last = c[CHUNK - 1:CHUNK, :]
    p_inv = jnp.exp(-c)
    p_end = jnp.exp(c_last - c)
    p_last = jnp.exp(c_last)
    rh = r * jnp.exp(c)
    bh = -kk * jnp.exp(c - log_decay)
    ka = kk * a
    lhs_t = _tiles(jnp.concatenate([bh, rh], axis=0))
    upd_t = _tiles(jnp.concatenate([ka * p_end, k * p_end], axis=0))
    ah_t, kh_t, v_t = _tiles(ka * p_inv), _tiles(k * p_inv), _tiles(v)
    nt = len(v_t)
    tiles = range(nt)

    states = [state_ref[t] for t in tiles]
    s_all = [_mm_nt(lhs_t[t], jnp.concatenate([_bd2(ah_t[t], lo_half), _bd2(kh_t[t], lo_half), _bf(states[t])],
                                              axis=0)) for t in tiles]
    s_a = [z[:, 0:TILE] for z in s_all]
    from_state = [z[:, 2 * TILE:3 * TILE] for z in s_all]
    bd_v = [_bd2(v_t[t], lo_half) for t in tiles]
    tri2 = jnp.concatenate([stril, tril], axis=0)
    yield
    uy = [from_state[t] + _mm(s_all[t][:, TILE:2 * TILE] * tri2, bd_v[t]) for t in tiles]
    u = [z[0:CHUNK] for z in uy]
    y_part = [z[CHUNK:] for z in uy]
    n_pow = [s_a[t][0:CHUNK] * stril for t in tiles]
    yield
    steps = CHUNK.bit_length() - 1
    for j in range(steps):
        u = [u[t] + _mm(n_pow[t], _bd2(u[t], lo_half)) for t in tiles]
        if j + 1 < steps:
            n_pow = [_mm(n_pow[t], _bd2(n_pow[t], lo_half)) for t in tiles]
        yield
    y = jnp.concatenate([y_part[t] + _mm(s_a[t][CHUNK:] * tril, _bd2(u[t], lo_half)) for t in tiles], axis=1)
    for t in tiles:
        upd = _mm_tn(jnp.concatenate([u[t], v_t[t]], axis=0), upd_t[t])
        state_ref[t] = states[t] * p_last[:, t * TILE:(t + 1) * TILE] + mask2f * upd
    yield
    yc = y - _segsum64(y, lo_half) * (1.0 / RWKV_HD)
    var = _segsum64(yc * yc, lo_half) * (1.0 / RWKV_HD)
    write((yc * lax.rsqrt(var + GN_EPS) * gain + bonus) * g)


def _mlstm_chunk(cols, write, tril, stril, tril64b, lo_half, convw, convb, gbias, gain, c_ref, n_ref, m_ref,
                 prev_ref):
    wp = MLSTM_WP
    nh = MLSTM_HEADS
    lane_g = lax.broadcasted_iota(jnp.int32, (1, TILE), 1)

    def col(x, h):
        return jnp.sum(jnp.where(lane_g == h, x, 0.0), axis=1, keepdims=True)

    def spread_s(cs):
        return jnp.concatenate([jnp.where(lo_half, cs[2 * p], cs[2 * p + 1]) for p in range(nh // 2)], axis=1)

    def spread_w(cs):
        return jnp.concatenate([jnp.broadcast_to(c, (c.shape[0], TILE)) for c in cs], axis=1)

    off = MLSTM_OFF
    qk_pre = cols(off, off + 2 * wp)
    prev8 = prev_ref[...]
    prev_ref[...] = qk_pre[CHUNK - SUB:CHUNK]
    conv = convb + qk_pre * convw[MLSTM_CONV - 1:MLSTM_CONV, :]
    for j in range(MLSTM_CONV - 1):
        conv = conv + _shift_rows(prev8, qk_pre, MLSTM_CONV - 1 - j) * convw[j:j + 1, :]
    qk = _silu(conv)
    q = qk[:, 0:wp]
    k = qk[:, wp:2 * wp] * (MLSTM_HD ** -0.5)
    v = cols(off + 2 * wp, off + 3 * wp)
    o = cols(off + 3 * wp, off + 4 * wp)
    gates = cols(off + 4 * wp, off + 4 * wp + TILE) + gbias
    i_pre = gates
    log_f = _log_sigmoid(pltpu.roll(gates, TILE - nh, 1))
    bcum = _cumsum_rows(tril64b, log_f)
    yield
    m_prev = m_ref[0:1, :]

    tril2 = jnp.concatenate([tril] * (nh // 2), axis=1)
    eye2 = tril2 - jnp.concatenate([stril] * (nh // 2), axis=1)
    bcum_c = [col(bcum, h) for h in range(nh)]
    src_c = [col(i_pre - bcum, h) for h in range(nh)]
    src_row = jnp.sum(spread_s(src_c) * eye2, axis=0, keepdims=True)
    d_mat = jnp.where(tril2 > 0.0, spread_s(bcum_c) + src_row, -jnp.inf)
    a_c = [bcum_c[h] + col(m_prev, h) for h in range(nh)]
    head_lanes = [lo_half, jnp.logical_not(lo_half)]
    d_tiles = _tiles(d_mat)
    m_c = [jnp.maximum(a_c[h], jnp.max(jnp.where(head_lanes[h % 2], d_tiles[h // 2], -jnp.inf),
                                       axis=1, keepdims=True)) for h in range(nh)]
    inter_c = [jnp.exp(a_c[h] - m_c[h]) for h in range(nh)]
    weights = jnp.exp(d_mat - spread_s(m_c))

    q_t, k_t, v_t = _tiles(q), _tiles(k), _tiles(v)
    zero = jnp.zeros((CHUNK, TILE), BF16)

    def pair_bd(z_t, p):
        a, b = _bf(z_t[2 * p]), _bf(z_t[2 * p + 1])
        return jnp.concatenate([jnp.concatenate([a, zero], axis=1), jnp.concatenate([zero, b], axis=1)], axis=0)

    pairs = range(nh // 2)
    heads = range(nh)
    c_states = [c_ref[h] for h in heads]
    raw = [_mm_nt(jnp.concatenate([q_t[2 * p], q_t[2 * p + 1]], axis=1), pair_bd(k_t, p)) for p in pairs]
    from_state = [_mm(q_t[h], c_states[h]) for h in heads]
    yield
    s_tiles = [raw[p] * weights[:, p * TILE:(p + 1) * TILE] for p in pairs]
    num_tiles = [z for p in pairs for z in _tiles(_mm(s_tiles[p], pair_bd(v_t, p)))]
    n_state = n_ref[0:1, :]
    qn_t = _tiles(q * n_state)
    b_last = bcum[CHUNK - 1:CHUNK, :]
    g_s = b_last - bcum + i_pre
    m_new = jnp.maximum(b_last + m_prev, jnp.max(g_s, axis=0, keepdims=True))
    ws = jnp.exp(g_s - m_new)
    carry = jnp.exp(b_last + m_prev - m_new)
    kw_t = [k_t[h] * col(ws, h) for h in heads]
    for h in heads:
        c_ref[h] = c_states[h] * col(carry, h) + _mm_tn(kw_t[h], v_t[h])
    n_ref[0:1, :] = (n_state * spread_w([col(carry, h) for h in heads])
                     + jnp.sum(jnp.concatenate(kw_t, axis=1), axis=0, keepdims=True))
    m_ref[0:1, :] = m_new
    yield
    pad_mask = jnp.where(lane_g < MLSTM_HD, 1.0, 0.0)
    hs = []
    for h in heads:
        num = num_tiles[h] + inter_c[h] * from_state[h]
        den = (jnp.sum(jnp.where(head_lanes[h % 2], s_tiles[h // 2], 0.0), axis=1, keepdims=True)
               + inter_c[h] * jnp.sum(qn_t[h], axis=1, keepdims=True))
        hc = num / jnp.maximum(jnp.abs(den), jnp.exp(-m_c[h]))
        mean = jnp.sum(hc, axis=1, keepdims=True) * (1.0 / MLSTM_HD)
        yc = (hc - mean) * pad_mask
        var = jnp.sum(yc * yc, axis=1, keepdims=True) * (1.0 / MLSTM_HD)
        hs.append(yc * lax.rsqrt(var + GN_EPS))
    write(jnp.concatenate(hs, axis=1) * gain * jax.nn.sigmoid(o))


def _mixer_constants(seq):
    h = jnp.arange(RET_HEADS, dtype=F32)
    log_g = jnp.log1p(-jnp.exp2(-5.0 - h))
    idx = jnp.arange(CHUNK, dtype=F32)
    diff = idx[:, None] - idx[None, :]
    intra = jnp.where(diff >= 0, jnp.exp(log_g[:, None, None] * jnp.maximum(diff, 0.0)), 0.0)
    intra = jnp.transpose(intra, (1, 0, 2)).reshape(CHUNK, RET_HEADS * CHUNK)
    q_decay = jnp.exp(log_g[:, None] * (idx + 1.0))
    k_decay = jnp.exp(log_g[:, None] * (CHUNK - 1.0 - idx))
    rep = lambda z: jnp.repeat(z.T, RET_HD, axis=1)
    dec = jnp.stack([intra, rep(q_decay), rep(k_decay)])
    cdec = jnp.repeat(jnp.exp(log_g * CHUNK), RET_HD)[None, :]

    inv = ROPE_BASE ** (-jnp.arange(0, RET_HD, 2, dtype=F32) / RET_HD)
    ang = jnp.arange(seq, dtype=F32)[:, None] * inv[None, :]
    cos = jnp.tile(jnp.cos(ang), (1, 2 * RET_HEADS))
    sin = jnp.tile(jnp.concatenate([-jnp.sin(ang), jnp.sin(ang)], axis=1), (1, RET_HEADS))

    half = np.arange(TILE) // CHUNK
    mask2 = (half[:, None] == half[None, :]).astype(np.float32)
    t = np.arange(CHUNK)
    tril = (t[:, None] >= t[None, :]).astype(np.float32)
    stril = (t[:, None] > t[None, :]).astype(np.float32)
    tri = np.stack([np.tile(tril, (1, 2)), np.tile(stril, (1, 2))])
    return dict(cos=cos, sin=sin, dec=dec, cdec=cdec, tri=jnp.asarray(tri), mask2f=jnp.asarray(mask2),
                tril64b=jnp.asarray(tril, BF16))


def _dense_chain(xin_ref, xres_ref, sh_ref, sc_ref, gm_ref, shf_ref, scf_ref, gf_ref, gin_ref, win_ref,
                 wout_ref, gff_ref, wi_ref, wo_ref, fin_ref, out_ref, proj_slot, mixed_slot, act_scr, x1_scr,
                 final):
    rows = range(MIX_BB)
    h = jnp.concatenate([_bf(_rms(xin_ref[b]) * gin_ref[...] * (1.0 + sc_ref[b]) + sh_ref[b]) for b in rows],
                        axis=0)
    for n0 in range(0, D_IN_PAD, IN_PIECE):
        proj_slot[:, n0:n0 + IN_PIECE] = jnp.dot(h, win_ref[:, n0:n0 + IN_PIECE], preferred_element_type=F32)
        yield
    att = jnp.dot(mixed_slot[...], wout_ref[...], preferred_element_type=F32)
    yield
    hs = []
    for b in rows:
        x1 = xres_ref[b] + gm_ref[b] * att[b * CHUNK:(b + 1) * CHUNK]
        x1_scr[b * CHUNK:(b + 1) * CHUNK, :] = x1
        hs.append(_bf(_rms(x1) * gff_ref[...] * (1.0 + scf_ref[b]) + shf_ref[b]))
    h = jnp.concatenate(hs, axis=0)
    for f0 in range(0, D_FF, FF_CHUNK):
        gate = jnp.dot(h, wi_ref[:, f0:f0 + FF_CHUNK], preferred_element_type=F32)
        up = jnp.dot(h, wi_ref[:, D_FF + f0:D_FF + f0 + FF_CHUNK], preferred_element_type=F32)
        act_scr[:, f0:f0 + FF_CHUNK] = _bf(_silu(gate) * up)
        yield
    pieces = []
    for n0 in range(0, D_MODEL, OUT_PIECE):
        pieces.append(jnp.dot(act_scr[...], wo_ref[:, n0:n0 + OUT_PIECE], preferred_element_type=F32))
        yield
    ffn = jnp.concatenate(pieces, axis=1)
    for b in rows:
        x2 = x1_scr[b * CHUNK:(b + 1) * CHUNK, :] + gf_ref[b] * ffn[b * CHUNK:(b + 1) * CHUNK]
        if final:
            x2 = _rms(x2) * fin_ref[...]
        out_ref[b] = x2


def _layer_kernel(xin_ref, xres_ref, sh_ref, sc_ref, gm_ref, shf_ref, scf_ref, gf_ref, cos_ref, sin_ref,
                  dec_ref, cdec_ref, tri_ref, mask2f_ref, tril64b_ref,
                  gret_ref, mu_ref, p384_ref, w2a2_ref, g2_ref, convw_ref, convb_ref, gbias_ref, gml_ref,
                  gin_ref, win_ref, wout_ref, gff_ref, wi_ref, wo_ref, fin_ref,
                  out_ref,
                  proj_scr, mixed_scr, act_scr, x1_scr, s_ret, s_rwkv, prev_rwkv, c_ml, n_ml, m_ml, prev_qk,
                  *, n_chunks, n_tiles, final):
    s = pl.program_id(0)
    mix_tile = jnp.clip(s - 1, 0, n_tiles - 1)
    slot_in = lax.rem(s, 2)
    slot_mix = 1 - slot_in

    @pl.when(s == 0)
    def _():
        proj_scr[1] = jnp.zeros(proj_scr.shape[1:], proj_scr.dtype)
        mixed_scr[0] = jnp.zeros(mixed_scr.shape[1:], mixed_scr.dtype)

    @pl.when(lax.rem(mix_tile, n_chunks) == 0)
    def _():
        for ref in (s_ret, s_rwkv, prev_rwkv, c_ml, n_ml, m_ml, prev_qk):
            ref[...] = jnp.zeros(ref.shape, ref.dtype)

    tril, stril = tri_ref[0], tri_ref[1]
    mask2f = mask2f_ref[...]
    tril64b = tril64b_ref[...]
    lo_half = lax.broadcasted_iota(jnp.int32, (1, TILE), 1) < TILE // 2
    cos = cos_ref[...]
    sin = sin_ref[...]
    proj_mix = proj_scr.at[slot_mix]
    mixed_mix = mixed_scr.at[slot_mix]
    dense = _dense_chain(xin_ref, xres_ref, sh_ref, sc_ref, gm_ref, shf_ref, scf_ref, gf_ref, gin_ref, win_ref,
                         wout_ref, gff_ref, wi_ref, wo_ref, fin_ref, out_ref, proj_scr.at[slot_in],
                         mixed_scr.at[slot_in], act_scr, x1_scr, final)
    mixers = []
    for bb in range(MIX_BB):
        r0 = bb * CHUNK
        cols = lambda lo, hi, r0=r0: proj_mix[r0:r0 + CHUNK, lo:hi]

        def writer(lo, hi, r0=r0):
            def write(y):
                mixed_mix[r0:r0 + CHUNK, lo:hi] = y.astype(mixed_mix.dtype)
            return write

        mixers.append(_rwkv_chunk(cols, writer(RET_W, RET_W + RWKV_W), tril, stril, tril64b, mask2f, lo_half,
                                  mu_ref[...], p384_ref[...], w2a2_ref[...], g2_ref[...],
                                  s_rwkv.at[bb], prev_rwkv.at[bb]))
        mixers.append(_retention_chunk(cols, writer(0, RET_W), cos, sin, dec_ref, cdec_ref[...], mask2f,
                                       lo_half, gret_ref[...], s_ret.at[bb]))
        mixers.append(_mlstm_chunk(cols, writer(RET_W + RWKV_W, D_MIX_PAD), tril, stril, tril64b, lo_half,
                                   convw_ref[...], convb_ref[...], gbias_ref[...], gml_ref[...],
                                   c_ml.at[bb], n_ml.at[bb], m_ml.at[bb], prev_qk.at[bb]))
    per_row = len(mixers) // MIX_BB
    first, second = mixers[:per_row * (MIX_BB // 2)], mixers[per_row * (MIX_BB // 2):]
    active = [dense] + first
    rnd = 0
    while active:
        if rnd == WAVE_DELAY:
            active = active + second
        active = [g for g in active if next(g, _DONE) is not _DONE]
        rnd += 1
    assert rnd > WAVE_DELAY


def _layer(x3, mod3, consts, lp, dense, final):
    batch, seq, d = x3.shape
    n_chunks = seq // CHUNK
    n_tiles = (batch // MIX_BB) * n_chunks
    rows = MIX_BB * CHUNK

    def tile(lag):
        def index(s):
            t = jnp.clip(s - lag, 0, n_tiles - 1)
            return t // n_chunks, lax.rem(t, n_chunks)
        return index

    x_spec = lambda lag: pl.BlockSpec((MIX_BB, CHUNK, d), lambda s: (*tile(lag)(s), 0))
    mod_spec = lambda lag, col: pl.BlockSpec((MIX_BB, 1, d), lambda s: (tile(lag)(s)[0], 0, col))
    rope_spec = pl.BlockSpec((CHUNK, RET_W), lambda s: (tile(1)(s)[1], 0))
    full = lambda a: pl.BlockSpec(a.shape, lambda s: (0,) * a.ndim, pipeline_mode=pl.Buffered(1))
    const_inputs = [consts["dec"], consts["cdec"], consts["tri"], consts["mask2f"], consts["tril64b"]]
    layer_inputs = [lp["gain_ret"], lp["mu"], lp["p384"], lp["w2a2"], lp["g2"], lp["convw"], lp["convb"],
                    lp["gbias"], lp["gain_ml"]]
    dense_inputs = [dense["gain_in"], dense["w_in"], dense["w_out"], dense["gain_ffn"], dense["wi"], dense["wo"],
                    dense["final_gain"]]
    return pl.pallas_call(
        functools.partial(_layer_kernel, n_chunks=n_chunks, n_tiles=n_tiles, final=final),
        grid=(n_tiles + 2,),
        in_specs=[x_spec(0), x_spec(2), mod_spec(0, 0), mod_spec(0, 1), mod_spec(2, 2), mod_spec(2, 3),
                  mod_spec(2, 4), mod_spec(2, 5), rope_spec, rope_spec]
                 + [full(a) for a in const_inputs + layer_inputs + dense_inputs],
        out_specs=x_spec(2),
        out_shape=jax.ShapeDtypeStruct((batch, seq, d), F32),
        scratch_shapes=[
            pltpu.VMEM((2, rows, D_IN_PAD), F32),
            pltpu.VMEM((2, rows, D_MIX_PAD), BF16),
            pltpu.VMEM((rows, D_FF), BF16),
            pltpu.VMEM((rows, d), F32),
            pltpu.VMEM((MIX_BB, RET_W // TILE, TILE, TILE), F32),
            pltpu.VMEM((MIX_BB, RWKV_W // TILE, TILE, TILE), F32),
            pltpu.VMEM((MIX_BB, SUB, RWKV_IN), F32),
            pltpu.VMEM((MIX_BB, MLSTM_HEADS, TILE, TILE), F32),
            pltpu.VMEM((MIX_BB, SUB, MLSTM_WP), F32),
            pltpu.VMEM((MIX_BB, SUB, TILE), F32),
            pltpu.VMEM((MIX_BB, SUB, 2 * MLSTM_WP), F32),
        ],
        compiler_params=pltpu.CompilerParams(dimension_semantics=("arbitrary",),
                                             vmem_limit_bytes=VMEM_LIMIT),
        name="layer",
    )(x3, x3, mod3, mod3, mod3, mod3, mod3, mod3, consts["cos"], consts["sin"],
      *const_inputs, *layer_inputs, *dense_inputs)


def _pad_heads_index(n_parts):
    idx = -np.ones(n_parts * MLSTM_WP, np.int64)
    for part in range(n_parts):
        for h in range(MLSTM_HEADS):
            dst = part * MLSTM_WP + h * TILE
            idx[dst:dst + MLSTM_HD] = part * MLSTM_W + h * MLSTM_HD + np.arange(MLSTM_HD)
    return idx


def _gather_cols(a, idx):
    return jnp.where(jnp.asarray(idx >= 0), a[..., np.maximum(idx, 0)], 0.0)


def _layer_params(l, mix_gn, rwkv_mu, rwkv_w0, rwkv_w2, rwkv_a0, rwkv_a2, rwkv_g2, rwkv_k_k,
                  rwkv_k_a, rwkv_r_k, mlstm_conv_w, mlstm_conv_b, mlstm_i_b, mlstm_f_b):
    gn = mix_gn[l]
    zeros = jnp.zeros((RWKV_W,), F32)
    p384 = jnp.stack([rwkv_w0[l], rwkv_a0[l], rwkv_k_k[l], rwkv_k_a[l], rwkv_r_k[l].reshape(RWKV_W),
                      gn[RET_W:RET_W + RWKV_W], zeros, zeros])
    w2a2 = jnp.zeros((LORA_W, 2 * RWKV_W), F32)
    w2a2 = w2a2.at[:RWKV_DECAY_LORA, :RWKV_W].set(rwkv_w2[l]).at[RWKV_DECAY_LORA:, RWKV_W:].set(rwkv_a2[l])
    gbias = jnp.zeros((1, TILE), F32)
    gbias = gbias.at[0, :MLSTM_HEADS].set(mlstm_i_b[l]).at[0, MLSTM_HEADS:2 * MLSTM_HEADS].set(mlstm_f_b[l])
    qk_idx = _pad_heads_index(2)
    convw = jnp.zeros((SUB, 2 * MLSTM_WP), F32).at[:MLSTM_CONV].set(_gather_cols(mlstm_conv_w[l], qk_idx))
    convb = _gather_cols(mlstm_conv_b[l], qk_idx)[None, :]
    gain_ml = _gather_cols(gn[RET_W + RWKV_W:], _pad_heads_index(1))[None, :]
    return dict(gain_ret=gn[None, :RET_W], mu=rwkv_mu[l][None, :], p384=p384, w2a2=_bf(w2a2),
                g2=_bf(rwkv_g2[l]), convw=convw, convb=convb, gbias=gbias, gain_ml=gain_ml)


def _pad_w_in(w):
    idx = _pad_heads_index(4)
    gates = -np.ones(TILE, np.int64)
    gates[:2 * MLSTM_HEADS] = 4 * MLSTM_W + np.arange(2 * MLSTM_HEADS)
    ml = _gather_cols(w[:, MLSTM_OFF:], np.concatenate([idx, gates]))
    return jnp.concatenate([w[:, :MLSTM_OFF], ml], axis=1)


def _pad_w_out(w):
    idx = _pad_heads_index(1)
    ml = _gather_cols(w[RET_W + RWKV_W:].T, idx).T
    return jnp.concatenate([w[:RET_W + RWKV_W], ml], axis=0)


def kernel(x, c, ada_w, ada_b, norm_mix, norm_ffn, w_in, mix_gn, rwkv_mu, rwkv_w0, rwkv_w2, rwkv_a0, rwkv_a2, rwkv_g2, rwkv_k_k, rwkv_k_a, rwkv_r_k, mlstm_conv_w, mlstm_conv_b, mlstm_i_b, mlstm_f_b, w_out, ffn_w_in, ffn_w_out, final_norm):
    batch, seq, d = x.shape
    depth = ada_w.shape[0]
    assert d == D_MODEL and seq % 256 == 0 and batch % MIX_BB == 0
    consts = _mixer_constants(seq)
    mod = _modulation(c, ada_w, ada_b)
    for l in range(depth):
        mod3 = mod[l].reshape(batch, 1, 6 * d)
        lp = _layer_params(l, mix_gn, rwkv_mu, rwkv_w0, rwkv_w2, rwkv_a0, rwkv_a2, rwkv_g2,
                           rwkv_k_k, rwkv_k_a, rwkv_r_k, mlstm_conv_w, mlstm_conv_b, mlstm_i_b, mlstm_f_b)
        dense = dict(gain_in=norm_mix[l][None, :], w_in=_bf(_pad_w_in(w_in[l])), w_out=_bf(_pad_w_out(w_out[l])),
                     gain_ffn=norm_ffn[l][None, :], wi=_bf(ffn_w_in[l]), wo=_bf(ffn_w_out[l]),
                     final_gain=final_norm[None, :])
        x = _layer(x, mod3, consts, lp, dense, final=(l == depth - 1))
    return x
```

```python
import functools

import numpy as np
import jax
import jax.numpy as jnp
from jax import lax
from jax.experimental import pallas as pl
from jax.experimental.pallas import tpu as pltpu

F32 = jnp.float32
BF16 = jnp.bfloat16

D_MODEL = 1024
CHUNK = 64
RET_HEADS, RET_HD = 4, 64
RET_W = RET_HEADS * RET_HD
RWKV_HEADS, RWKV_HD = 6, 64
RWKV_W = RWKV_HEADS * RWKV_HD
MLSTM_HEADS, MLSTM_HD = 4, 96
MLSTM_W = MLSTM_HEADS * MLSTM_HD
D_MIX = RET_W + RWKV_W + MLSTM_W
RWKV_DECAY_LORA, RWKV_ICL_LORA, RWKV_GATE_LORA = 64, 64, 128
MLSTM_CONV = 4
D_FF = 2816
ROPE_BASE = 10000.0
RMS_EPS = 1e-6
GN_EPS = 1e-5

RET_IN = 4 * RET_W
RWKV_IN = 3 * RWKV_W + RWKV_DECAY_LORA + RWKV_ICL_LORA + RWKV_GATE_LORA
MLSTM_IN = 4 * MLSTM_W + 2 * MLSTM_HEADS
D_IN = RET_IN + RWKV_IN + MLSTM_IN
TILE = 128
SUB = 8
MLSTM_WP = MLSTM_HEADS * TILE
MLSTM_IN_PAD = 4 * MLSTM_WP + TILE
D_IN_PAD = RET_IN + RWKV_IN + MLSTM_IN_PAD
D_MIX_PAD = RET_W + RWKV_W + MLSTM_WP
RWKV_OFF = RET_IN
MLSTM_OFF = RET_IN + RWKV_IN
LORA_W = RWKV_DECAY_LORA + RWKV_ICL_LORA
FF_CHUNK = 256
IN_PIECE = 512
OUT_PIECE = 256
MIX_BB = 4

WAVE_DELAY = 7
VMEM_LIMIT = 60000 * 1024
_DONE = object()


def _bf(x):
    return x.astype(BF16)


def _mm(a, b):
    return jnp.dot(_bf(a), _bf(b), preferred_element_type=F32)


def _mm_nt(a, b):
    return lax.dot_general(_bf(a), _bf(b), (((1,), (1,)), ((), ())), preferred_element_type=F32)


def _mm_tn(a, b):
    return lax.dot_general(_bf(a), _bf(b), (((0,), (0,)), ((), ())), preferred_element_type=F32)


def _split3(x):
    hi = _bf(x)
    r = x - hi.astype(F32)
    mid = _bf(r)
    lo = _bf(r - mid.astype(F32))
    return hi, mid, lo


def _tiles(x):
    return [x[:, t * TILE:(t + 1) * TILE] for t in range(x.shape[1] // TILE)]


def _cumsum_rows(tril_b, x):
    w = x.shape[1]
    r = jnp.dot(tril_b, jnp.concatenate(_split3(x), axis=1), preferred_element_type=F32)
    return r[:, 0:w] + r[:, w:2 * w] + r[:, 2 * w:3 * w]


def _segsum64(x, lo_half):
    outs = []
    for t in _tiles(x):
        lo = jnp.sum(jnp.where(lo_half, t, 0.0), axis=1, keepdims=True)
        hi = jnp.sum(jnp.where(lo_half, 0.0, t), axis=1, keepdims=True)
        outs.append(jnp.where(lo_half, lo, hi))
    return jnp.concatenate(outs, axis=1)


def _bd2(y, lo_half):
    yb = _bf(y)
    zero = jnp.zeros_like(yb)
    return jnp.concatenate([jnp.where(lo_half, yb, zero), jnp.where(lo_half, zero, yb)], axis=0)


def _log_sigmoid(x):
    return jnp.minimum(x, 0.0) - jnp.log(1.0 + jnp.exp(-jnp.abs(x)))


def _silu(x):
    return x * jax.nn.sigmoid(x)


def _rms(x):
    return x * lax.rsqrt(jnp.mean(x * x, axis=-1, keepdims=True) + RMS_EPS)


def _shift_rows(prev8, cur, k):
    rolled = pltpu.roll(cur, k, 0)
    row8 = lax.broadcasted_iota(jnp.int32, (SUB, 1), 0)
    first = jnp.where(row8 < k, pltpu.roll(prev8, k, 0), rolled[0:SUB])
    return jnp.concatenate([first, rolled[SUB:]], axis=0)


def _mod_kernel(c_ref, w_ref, b_ref, o_ref):
    cond = _silu(c_ref[...])
    o_ref[0] = _mm(cond, w_ref[0]) + b_ref[0]


def _modulation(c, ada_w, ada_b):
    depth, d, n = ada_w.shape
    b = c.shape[0]
    tn = 1536
    return pl.pallas_call(
        _mod_kernel,
        grid=(depth, n // tn),
        in_specs=[
            pl.BlockSpec((b, d), lambda l, j: (0, 0)),
            pl.BlockSpec((1, d, tn), lambda l, j: (l, 0, j)),
            pl.BlockSpec((1, 1, tn), lambda l, j: (l, 0, j)),
        ],
        out_specs=pl.BlockSpec((1, b, tn), lambda l, j: (l, 0, j)),
        out_shape=jax.ShapeDtypeStruct((depth, b, n), F32),
        compiler_params=pltpu.CompilerParams(dimension_semantics=("arbitrary", "arbitrary"),
                                             vmem_limit_bytes=VMEM_LIMIT),
        name="adaln_modulation",
    )(c, ada_w, ada_b.reshape(depth, 1, n))


def _retention_chunk(cols, write, cos, sin, dec_ref, cdec, mask2f, lo_half, gain, state_ref):
    w = RET_W
    lane = lax.broadcasted_iota(jnp.int32, (CHUNK, w), 1)
    first_half = (lane & (RET_HD - 1)) < RET_HD // 2

    def rope(z):
        swapped = jnp.where(first_half, pltpu.roll(z, w - RET_HD // 2, 1), pltpu.roll(z, RET_HD // 2, 1))
        return z * cos + swapped * sin

    q = rope(cols(0, w))
    k = rope(cols(w, 2 * w)) * (RET_HD ** -0.5)
    v = cols(2 * w, 3 * w)
    g = cols(3 * w, 4 * w)
    intra, q_dec = _tiles(dec_ref[0]), _tiles(dec_ref[1])
    q_t, k_t, v_t, kd_t = _tiles(q), _tiles(k), _tiles(v), _tiles(k * dec_ref[2])
    nt = len(q_t)
    states = [state_ref[t] for t in range(nt)]
    scores = [_mm_nt(q_t[t], _bd2(k_t[t], lo_half)) * intra[t] for t in range(nt)]
    inter = [_mm(q_t[t], states[t]) * q_dec[t] for t in range(nt)]
    yield
    y = jnp.concatenate([_mm(scores[t], _bd2(v_t[t], lo_half)) + inter[t] for t in range(nt)], axis=1)
    for t in range(nt):
        state_ref[t] = states[t] * cdec[:, t * TILE:(t + 1) * TILE] + mask2f * _mm_tn(kd_t[t], v_t[t])
    yield
    yc = y - _segsum64(y, lo_half) * (1.0 / RET_HD)
    var = _segsum64(yc * yc, lo_half) * (1.0 / RET_HD)
    write(yc * lax.rsqrt(var + GN_EPS) * gain * _silu(g))


def _rwkv_chunk(cols, write, tril, stril, tril64b, mask2f, lo_half, mu, p384, w2a2, g2, state_ref, prev_ref):
    w = RWKV_W
    x = cols(RWKV_OFF, RWKV_OFF + RWKV_IN)
    prev = _shift_rows(prev_ref[...], x, 1)
    prev_ref[...] = x[CHUNK - SUB:CHUNK]
    xs = x + (prev - x) * mu
    r = xs[:, 0:w]
    k = xs[:, w:2 * w]
    v = xs[:, 2 * w:3 * w]
    lora = xs[:, 3 * w:3 * w + LORA_W]
    gl = xs[:, 3 * w + LORA_W:RWKV_IN]
    w0, a0, k_k, k_a, r_k, gain = (p384[i:i + 1, :] for i in range(6))

    lane = lax.broadcasted_iota(jnp.int32, (CHUNK, LORA_W), 1)
    wa = _mm(jnp.where(lane < RWKV_DECAY_LORA, jnp.tanh(lora), lora), w2a2)
    g = _mm(jax.nn.sigmoid(gl), g2)
    kk = k * k_k
    kk_sq = _segsum64(kk * kk, lo_half)
    yield
    log_decay = -jnp.exp(_log_sigmoid(w0 + wa[:, 0:w]) - 0.5)
    a = jax.nn.sigmoid(a0 + wa[:, w:2 * w])
    kk = kk / jnp.maximum(jnp.sqrt(kk_sq), 1e-12)
    k = k * (1.0 + (a - 1.0) * k_a)
    bonus = _segsum64(r * k * r_k, lo_half) * v
    c = _cumsum_rows(tril64b, log_decay)
    yield
    c_last = c[CHUNK - 1:CHUNK, :]
    p_inv = jnp.exp(-c)
    p_end = jnp.exp(c_last - c)
    p_last = jnp.exp(c_last)
    rh = r * jnp.exp(c)
    bh = -kk * jnp.exp(c - log_decay)
    ka = kk * a
    lhs_t = _tiles(jnp.concatenate([bh, rh], axis=0))
    upd_t = _tiles(jnp.concatenate([ka * p_end, k * p_end], axis=0))
    ah_t, kh_t, v_t = _tiles(ka * p_inv), _tiles(k * p_inv), _tiles(v)
    nt = len(v_t)
    tiles = range(nt)

    states = [state_ref[t] for t in tiles]
    s_all = [_mm_nt(lhs_t[t], jnp.concatenate([_bd2(ah_t[t], lo_half), _bd2(kh_t[t], lo_half), _bf(states[t])],
                                              axis=0)) for t in tiles]
    s_a = [z[:, 0:TILE] for z in s_all]
    from_state = [z[:, 2 * TILE:3 * TILE] for z in s_all]
    bd_v = [_bd2(v_t[t], lo_half) for t in tiles]
    tri2 = jnp.concatenate([stril, tril], axis=0)
    yield
    uy = [from_state[t] + _mm(s_all[t][:, TILE:2 * TILE] * tri2, bd_v[t]) for t in tiles]
    u = [z[0:CHUNK] for z in uy]
    y_part = [z[CHUNK:] for z in uy]
    n_pow = [s_a[t][0:CHUNK] * stril for t in tiles]
    yield
    steps = CHUNK.bit_length() - 1
    for j in range(steps):
        u = [u[t] + _mm(n_pow[t], _bd2(u[t], lo_half)) for t in tiles]
        if j + 1 < steps:
            n_pow = [_mm(n_pow[t], _bd2(n_pow[t], lo_half)) for t in tiles]
        yield
    y = jnp.concatenate([y_part[t] + _mm(s_a[t][CHUNK:] * tril, _bd2(u[t], lo_half)) for t in tiles], axis=1)
    for t in tiles:
        upd = _mm_tn(jnp.concatenate([u[t], v_t[t]], axis=0), upd_t[t])
        state_ref[t] = states[t] * p_last[:, t * TILE:(t + 1) * TILE] + mask2f * upd
    yield
    yc = y - _segsum64(y, lo_half) * (1.0 / RWKV_HD)
    var = _segsum64(yc * yc, lo_half) * (1.0 / RWKV_HD)
    write((yc * lax.rsqrt(var + GN_EPS) * gain + bonus) * g)


def _mlstm_chunk(cols, write, tril, stril, tril64b, lo_half, convw, convb, gbias, gain, c_ref, n_ref, m_ref,
                 prev_ref):
    wp = MLSTM_WP
    nh = MLSTM_HEADS
    lane_g = lax.broadcasted_iota(jnp.int32, (1, TILE), 1)

    def col(x, h):
        return jnp.sum(jnp.where(lane_g == h, x, 0.0), axis=1, keepdims=True)

    def spread_s(cs):
        return jnp.concatenate([jnp.where(lo_half, cs[2 * p], cs[2 * p + 1]) for p in range(nh // 2)], axis=1)

    def spread_w(cs):
        return jnp.concatenate([jnp.broadcast_to(c, (c.shape[0], TILE)) for c in cs], axis=1)

    off = MLSTM_OFF
    qk_pre = cols(off, off + 2 * wp)
    prev8 = prev_ref[...]
    prev_ref[...] = qk_pre[CHUNK - SUB:CHUNK]
    conv = convb + qk_pre * convw[MLSTM_CONV - 1:MLSTM_CONV, :]
    for j in range(MLSTM_CONV - 1):
        conv = conv + _shift_rows(prev8, qk_pre, MLSTM_CONV - 1 - j) * convw[j:j + 1, :]
    qk = _silu(conv)
    q = qk[:, 0:wp]
    k = qk[:, wp:2 * wp] * (MLSTM_HD ** -0.5)
    v = cols(off + 2 * wp, off + 3 * wp)
    o = cols(off + 3 * wp, off + 4 * wp)
    gates = cols(off + 4 * wp, off + 4 * wp + TILE) + gbias
    i_pre = gates
    log_f = _log_sigmoid(pltpu.roll(gates, TILE - nh, 1))
    bcum = _cumsum_rows(tril64b, log_f)
    yield
    m_prev = m_ref[0:1, :]

    tril2 = jnp.concatenate([tril] * (nh // 2), axis=1)
    eye2 = tril2 - jnp.concatenate([stril] * (nh // 2), axis=1)
    bcum_c = [col(bcum, h) for h in range(nh)]
    src_c = [col(i_pre - bcum, h) for h in range(nh)]
    src_row = jnp.sum(spread_s(src_c) * eye2, axis=0, keepdims=True)
    d_mat = jnp.where(tril2 > 0.0, spread_s(bcum_c) + src_row, -jnp.inf)
    a_c = [bcum_c[h] + col(m_prev, h) for h in range(nh)]
    head_lanes = [lo_half, jnp.logical_not(lo_half)]
    d_tiles = _tiles(d_mat)
    m_c = [jnp.maximum(a_c[h], jnp.max(jnp.where(head_lanes[h % 2], d_tiles[h // 2], -jnp.inf),
                                       axis=1, keepdims=True)) for h in range(nh)]
    inter_c = [jnp.exp(a_c[h] - m_c[h]) for h in range(nh)]
    weights = jnp.exp(d_mat - spread_s(m_c))

    q_t, k_t, v_t = _tiles(q), _tiles(k), _tiles(v)
    zero = jnp.zeros((CHUNK, TILE), BF16)

    def pair_bd(z_t, p):
        a, b = _bf(z_t[2 * p]), _bf(z_t[2 * p + 1])
        return jnp.concatenate([jnp.concatenate([a, zero], axis=1), jnp.concatenate([zero, b], axis=1)], axis=0)

    pairs = range(nh // 2)
    heads = range(nh)
    c_states = [c_ref[h] for h in heads]
    raw = [_mm_nt(jnp.concatenate([q_t[2 * p], q_t[2 * p + 1]], axis=1), pair_bd(k_t, p)) for p in pairs]
    from_state = [_mm(q_t[h], c_states[h]) for h in heads]
    yield
    s_tiles = [raw[p] * weights[:, p * TILE:(p + 1) * TILE] for p in pairs]
    num_tiles = [z for p in pairs for z in _tiles(_mm(s_tiles[p], pair_bd(v_t, p)))]
    n_state = n_ref[0:1, :]
    qn_t = _tiles(q * n_state)
    b_last = bcum[CHUNK - 1:CHUNK, :]
    g_s = b_last - bcum + i_pre
    m_new = jnp.maximum(b_last + m_prev, jnp.max(g_s, axis=0, keepdims=True))
    ws = jnp.exp(g_s - m_new)
    carry = jnp.exp(b_last + m_prev - m_new)
    kw_t = [k_t[h] * col(ws, h) for h in heads]
    for h in heads:
        c_ref[h] = c_states[h] * col(carry, h) + _mm_tn(kw_t[h], v_t[h])
    n_ref[0:1, :] = (n_state * spread_w([col(carry, h) for h in heads])
                     + jnp.sum(jnp.concatenate(kw_t, axis=1), axis=0, keepdims=True))
    m_ref[0:1, :] = m_new
    yield
    pad_mask = jnp.where(lane_g < MLSTM_HD, 1.0, 0.0)
    hs = []
    for h in heads:
        num = num_tiles[h] + inter_c[h] * from_state[h]
        den = (jnp.sum(jnp.where(head_lanes[h % 2], s_tiles[h // 2], 0.0), axis=1, keepdims=True)
               + inter_c[h] * jnp.sum(qn_t[h], axis=1, keepdims=True))
        hc = num / jnp.maximum(jnp.abs(den), jnp.exp(-m_c[h]))
        mean = jnp.sum(hc, axis=1, keepdims=True) * (1.0 / MLSTM_HD)
        yc = (hc - mean) * pad_mask
        var = jnp.sum(yc * yc, axis=1, keepdims=True) * (1.0 / MLSTM_HD)
        hs.append(yc * lax.rsqrt(var + GN_EPS))
    write(jnp.concatenate(hs, axis=1) * gain * jax.nn.sigmoid(o))


def _mixer_constants(seq):
    h = jnp.arange(RET_HEADS, dtype=F32)
    log_g = jnp.log1p(-jnp.exp2(-5.0 - h))
    idx = jnp.arange(CHUNK, dtype=F32)
    diff = idx[:, None] - idx[None, :]
    intra = jnp.where(diff >= 0, jnp.exp(log_g[:, None, None] * jnp.maximum(diff, 0.0)), 0.0)
    intra = jnp.transpose(intra, (1, 0, 2)).reshape(CHUNK, RET_HEADS * CHUNK)
    q_decay = jnp.exp(log_g[:, None] * (idx + 1.0))
    k_decay = jnp.exp(log_g[:, None] * (CHUNK - 1.0 - idx))
    rep = lambda z: jnp.repeat(z.T, RET_HD, axis=1)
    dec = jnp.stack([intra, rep(q_decay), rep(k_decay)])
    cdec = jnp.repeat(jnp.exp(log_g * CHUNK), RET_HD)[None, :]

    inv = ROPE_BASE ** (-jnp.arange(0, RET_HD, 2, dtype=F32) / RET_HD)
    ang = jnp.arange(seq, dtype=F32)[:, None] * inv[None, :]
    cos = jnp.tile(jnp.cos(ang), (1, 2 * RET_HEADS))
    sin = jnp.tile(jnp.concatenate([-jnp.sin(ang), jnp.sin(ang)], axis=1), (1, RET_HEADS))

    half = np.arange(TILE) // CHUNK
    mask2 = (half[:, None] == half[None, :]).astype(np.float32)
    t = np.arange(CHUNK)
    tril = (t[:, None] >= t[None, :]).astype(np.float32)
    stril = (t[:, None] > t[None, :]).astype(np.float32)
    tri = np.stack([np.tile(tril, (1, 2)), np.tile(stril, (1, 2))])
    return dict(cos=cos, sin=sin, dec=dec, cdec=cdec, tri=jnp.asarray(tri), mask2f=jnp.asarray(mask2),
                tril64b=jnp.asarray(tril, BF16))


def _dense_chain(xin_ref, xres_ref, sh_ref, sc_ref, gm_ref, shf_ref, scf_ref, gf_ref, gin_ref, win_ref,
                 wout_ref, gff_ref, wi_ref, wo_ref, fin_ref, out_ref, proj_slot, mixed_slot, act_scr, x1_scr,
                 final):
    rows = range(MIX_BB)
    h = jnp.concatenate([_bf(_rms(xin_ref[b]) * gin_ref[...] * (1.0 + sc_ref[b]) + sh_ref[b]) for b in rows],
                        axis=0)
    for n0 in range(0, D_IN_PAD, IN_PIECE):
        proj_slot[:, n0:n0 + IN_PIECE] = jnp.dot(h, win_ref[:, n0:n0 + IN_PIECE], preferred_element_type=F32)
        yield
    att = jnp.dot(mixed_slot[...], wout_ref[...], preferred_element_type=F32)
    yield
    hs = []
    for b in rows:
        x1 = xres_ref[b] + gm_ref[b] * att[b * CHUNK:(b + 1) * CHUNK]
        x1_scr[b * CHUNK:(b + 1) * CHUNK, :] = x1
        hs.append(_bf(_rms(x1) * gff_ref[...] * (1.0 + scf_ref[b]) + shf_ref[b]))
    h = jnp.concatenate(hs, axis=0)
    for f0 in range(0, D_FF, FF_CHUNK):
        gate = jnp.dot(h, wi_ref[:, f0:f0 + FF_CHUNK], preferred_element_type=F32)
        up = jnp.dot(h, wi_ref[:, D_FF + f0:D_FF + f0 + FF_CHUNK], preferred_element_type=F32)
        act_scr[:, f0:f0 + FF_CHUNK] = _bf(_silu(gate) * up)
        yield
    pieces = []
    for n0 in range(0, D_MODEL, OUT_PIECE):
        pieces.append(jnp.dot(act_scr[...], wo_ref[:, n0:n0 + OUT_PIECE], preferred_element_type=F32))
        yield
    ffn = jnp.concatenate(pieces, axis=1)
    for b in rows:
        x2 = x1_scr[b * CHUNK:(b + 1) * CHUNK, :] + gf_ref[b] * ffn[b * CHUNK:(b + 1) * CHUNK]
        if final:
            x2 = _rms(x2) * fin_ref[...]
        out_ref[b] = x2


def _layer_kernel(xin_ref, xres_ref, sh_ref, sc_ref, gm_ref, shf_ref, scf_ref, gf_ref, cos_ref, sin_ref,
                  dec_ref, cdec_ref, tri_ref, mask2f_ref, tril64b_ref,
                  gret_ref, mu_ref, p384_ref, w2a2_ref, g2_ref, convw_ref, convb_ref, gbias_ref, gml_ref,
                  gin_ref, win_ref, wout_ref, gff_ref, wi_ref, wo_ref, fin_ref,
                  out_ref,
                  proj_scr, mixed_scr, act_scr, x1_scr, s_ret, s_rwkv, prev_rwkv, c_ml, n_ml, m_ml, prev_qk,
                  *, n_chunks, n_tiles, final):
    s = pl.program_id(0)
    mix_tile = jnp.clip(s - 1, 0, n_tiles - 1)
    slot_in = lax.rem(s, 2)
    slot_mix = 1 - slot_in

    @pl.when(s == 0)
    def _():
        proj_scr[1] = jnp.zeros(proj_scr.shape[1:], proj_scr.dtype)
        mixed_scr[0] = jnp.zeros(mixed_scr.shape[1:], mixed_scr.dtype)

    @pl.when(lax.rem(mix_tile, n_chunks) == 0)
    def _():
        for ref in (s_ret, s_rwkv, prev_rwkv, c_ml, n_ml, m_ml, prev_qk):
            ref[...] = jnp.zeros(ref.shape, ref.dtype)

    tril, stril = tri_ref[0], tri_ref[1]
    mask2f = mask2f_ref[...]
    tril64b = tril64b_ref[...]
    lo_half = lax.broadcasted_iota(jnp.int32, (1, TILE), 1) < TILE // 2
    cos = cos_ref[...]
    sin = sin_ref[...]
    proj_mix = proj_scr.at[slot_mix]
    mixed_mix = mixed_scr.at[slot_mix]
    dense = _dense_chain(xin_ref, xres_ref, sh_ref, sc_ref, gm_ref, shf_ref, scf_ref, gf_ref, gin_ref, win_ref,
                         wout_ref, gff_ref, wi_ref, wo_ref, fin_ref, out_ref, proj_scr.at[slot_in],
                         mixed_scr.at[slot_in], act_scr, x1_scr, final)
    mixers = []
    for bb in range(MIX_BB):
        r0 = bb * CHUNK
        cols = lambda lo, hi, r0=r0: proj_mix[r0:r0 + CHUNK, lo:hi]

        def writer(lo, hi, r0=r0):
            def write(y):
                mixed_mix[r0:r0 + CHUNK, lo:hi] = y.astype(mixed_mix.dtype)
            return write

        mixers.append(_rwkv_chunk(cols, writer(RET_W, RET_W + RWKV_W), tril, stril, tril64b, mask2f, lo_half,
                                  mu_ref[...], p384_ref[...], w2a2_ref[...], g2_ref[...],
                                  s_rwkv.at[bb], prev_rwkv.at[bb]))
        mixers.append(_retention_chunk(cols, writer(0, RET_W), cos, sin, dec_ref, cdec_ref[...], mask2f,
                                       lo_half, gret_ref[...], s_ret.at[bb]))
        mixers.append(_mlstm_chunk(cols, writer(RET_W + RWKV_W, D_MIX_PAD), tril, stril, tril64b, lo_half,
                                   convw_ref[...], convb_ref[...], gbias_ref[...], gml_ref[...],
                                   c_ml.at[bb], n_ml.at[bb], m_ml.at[bb], prev_qk.at[bb]))
    per_row = len(mixers) // MIX_BB
    first, second = mixers[:per_row * (MIX_BB // 2)], mixers[per_row * (MIX_BB // 2):]
    active = [dense] + first
    rnd = 0
    while active:
        if rnd == WAVE_DELAY:
            active = active + second
        active = [g for g in active if next(g, _DONE) is not _DONE]
        rnd += 1
    assert rnd > WAVE_DELAY


def _layer(x3, mod, consts, layer_params, final_gain, layer, final):
    batch, seq, d = x3.shape
    n_chunks = seq // CHUNK
    n_tiles = (batch // MIX_BB) * n_chunks
    rows = MIX_BB * CHUNK

    def tile(lag):
        def index(s):
            t = jnp.clip(s - lag, 0, n_tiles - 1)
            return t // n_chunks, lax.rem(t, n_chunks)
        return index

    x_spec = lambda lag: pl.BlockSpec((MIX_BB, CHUNK, d), lambda s: (*tile(lag)(s), 0))
    mod_spec = lambda lag, col: pl.BlockSpec((None, MIX_BB, 1, d), lambda s: (layer, tile(lag)(s)[0], 0, col))
    rope_spec = pl.BlockSpec((CHUNK, RET_W), lambda s: (tile(1)(s)[1], 0))
    full = lambda a: pl.BlockSpec(a.shape, lambda s: (0,) * a.ndim, pipeline_mode=pl.Buffered(1))
    of_layer = lambda a: pl.BlockSpec((None,) + a.shape[1:], lambda s: (layer,) + (0,) * (a.ndim - 1),
                                      pipeline_mode=pl.Buffered(1))
    const_inputs = [consts["dec"], consts["cdec"], consts["tri"], consts["mask2f"], consts["tril64b"]]
    return pl.pallas_call(
        functools.partial(_layer_kernel, n_chunks=n_chunks, n_tiles=n_tiles, final=final),
        grid=(n_tiles + 2,),
        in_specs=[x_spec(0), x_spec(2), mod_spec(0, 0), mod_spec(0, 1), mod_spec(2, 2), mod_spec(2, 3),
                  mod_spec(2, 4), mod_spec(2, 5), rope_spec, rope_spec]
                 + [full(a) for a in const_inputs] + [of_layer(a) for a in layer_params] + [full(final_gain)],
        out_specs=x_spec(2),
        out_shape=jax.ShapeDtypeStruct((batch, seq, d), F32),
        scratch_shapes=[
            pltpu.VMEM((2, rows, D_IN_PAD), F32),
            pltpu.VMEM((2, rows, D_MIX_PAD), BF16),
            pltpu.VMEM((rows, D_FF), BF16),
            pltpu.VMEM((rows, d), F32),
            pltpu.VMEM((MIX_BB, RET_W // TILE, TILE, TILE), F32),
            pltpu.VMEM((MIX_BB, RWKV_W // TILE, TILE, TILE), F32),
            pltpu.VMEM((MIX_BB, SUB, RWKV_IN), F32),
            pltpu.VMEM((MIX_BB, MLSTM_HEADS, TILE, TILE), F32),
            pltpu.VMEM((MIX_BB, SUB, MLSTM_WP), F32),
            pltpu.VMEM((MIX_BB, SUB, TILE), F32),
            pltpu.VMEM((MIX_BB, SUB, 2 * MLSTM_WP), F32),
        ],
        compiler_params=pltpu.CompilerParams(dimension_semantics=("arbitrary",),
                                             vmem_limit_bytes=VMEM_LIMIT),
        name="layer",
    )(x3, x3, mod, mod, mod, mod, mod, mod, consts["cos"], consts["sin"],
      *const_inputs, *layer_params, final_gain)


def _pad_heads_index(n_parts):
    idx = -np.ones(n_parts * MLSTM_WP, np.int64)
    for part in range(n_parts):
        for h in range(MLSTM_HEADS):
            dst = part * MLSTM_WP + h * TILE
            idx[dst:dst + MLSTM_HD] = part * MLSTM_W + h * MLSTM_HD + np.arange(MLSTM_HD)
    return idx


def _gather_cols(a, idx):
    return jnp.where(jnp.asarray(idx >= 0), a[..., np.maximum(idx, 0)], 0.0)


def _mixer_params(mix_gn, rwkv_mu, rwkv_w0, rwkv_w2, rwkv_a0, rwkv_a2, rwkv_g2, rwkv_k_k, rwkv_k_a, rwkv_r_k,
                  mlstm_conv_w, mlstm_conv_b, mlstm_i_b, mlstm_f_b):
    depth = mix_gn.shape[0]
    zeros = jnp.zeros((depth, RWKV_W), F32)
    p384 = jnp.stack([rwkv_w0, rwkv_a0, rwkv_k_k, rwkv_k_a, rwkv_r_k.reshape(depth, RWKV_W),
                      mix_gn[:, RET_W:RET_W + RWKV_W], zeros, zeros], axis=1)
    w2a2 = jnp.zeros((depth, LORA_W, 2 * RWKV_W), F32)
    w2a2 = w2a2.at[:, :RWKV_DECAY_LORA, :RWKV_W].set(rwkv_w2).at[:, RWKV_DECAY_LORA:, RWKV_W:].set(rwkv_a2)
    gbias = jnp.zeros((depth, 1, TILE), F32)
    gbias = gbias.at[:, 0, :MLSTM_HEADS].set(mlstm_i_b).at[:, 0, MLSTM_HEADS:2 * MLSTM_HEADS].set(mlstm_f_b)
    qk_idx = _pad_heads_index(2)
    convw = jnp.zeros((depth, SUB, 2 * MLSTM_WP), F32).at[:, :MLSTM_CONV].set(_gather_cols(mlstm_conv_w, qk_idx))
    convb = _gather_cols(mlstm_conv_b, qk_idx)[:, None, :]
    gain_ml = _gather_cols(mix_gn[:, RET_W + RWKV_W:], _pad_heads_index(1))[:, None, :]
    return [mix_gn[:, None, :RET_W], rwkv_mu[:, None, :], p384, _bf(w2a2), _bf(rwkv_g2), convw, convb, gbias,
            gain_ml]


def _pad_w_in(w):
    idx = _pad_heads_index(4)
    gates = -np.ones(TILE, np.int64)
    gates[:2 * MLSTM_HEADS] = 4 * MLSTM_W + np.arange(2 * MLSTM_HEADS)
    ml = _gather_cols(w[..., MLSTM_OFF:], np.concatenate([idx, gates]))
    return jnp.concatenate([w[..., :MLSTM_OFF], ml], axis=-1)


def _pad_w_out(w):
    idx = _pad_heads_index(1)
    ml = jnp.where(jnp.asarray(idx >= 0)[:, None], w[:, RET_W + RWKV_W + np.maximum(idx, 0)], 0.0)
    return jnp.concatenate([w[:, :RET_W + RWKV_W], ml], axis=1)


def kernel(x, c, ada_w, ada_b, norm_mix, norm_ffn, w_in, mix_gn, rwkv_mu, rwkv_w0, rwkv_w2, rwkv_a0, rwkv_a2, rwkv_g2, rwkv_k_k, rwkv_k_a, rwkv_r_k, mlstm_conv_w, mlstm_conv_b, mlstm_i_b, mlstm_f_b, w_out, ffn_w_in, ffn_w_out, final_norm):
    batch, seq, d = x.shape
    depth = ada_w.shape[0]
    assert d == D_MODEL and seq % 256 == 0 and batch % MIX_BB == 0
    consts = _mixer_constants(seq)
    mod = _modulation(c, ada_w, ada_b).reshape(depth, batch, 1, 6 * d)
    mixer_params = _mixer_params(mix_gn, rwkv_mu, rwkv_w0, rwkv_w2, rwkv_a0, rwkv_a2, rwkv_g2, rwkv_k_k, rwkv_k_a,
                                 rwkv_r_k, mlstm_conv_w, mlstm_conv_b, mlstm_i_b, mlstm_f_b)
    dense_params = [norm_mix[:, None, :], _pad_w_in(_bf(w_in)), _pad_w_out(_bf(w_out)), norm_ffn[:, None, :],
                    _bf(ffn_w_in), _bf(ffn_w_out)]
    for l in range(depth):
        x = _layer(x, mod, consts, mixer_params + dense_params, final_norm[None, :], l, final=(l == depth - 1))
    return x
```

```python
import functools

import numpy as np
import jax
import jax.numpy as jnp
from jax import lax
from jax.experimental import pallas as pl
from jax.experimental.pallas import tpu as pltpu

F32 = jnp.float32
BF16 = jnp.bfloat16

D_MODEL = 1024
CHUNK = 64
RET_HEADS, RET_HD = 4, 64
RET_W = RET_HEADS * RET_HD
RWKV_HEADS, RWKV_HD = 6, 64
RWKV_W = RWKV_HEADS * RWKV_HD
MLSTM_HEADS, MLSTM_HD = 4, 96
MLSTM_W = MLSTM_HEADS * MLSTM_HD
D_MIX = RET_W + RWKV_W + MLSTM_W
RWKV_DECAY_LORA, RWKV_ICL_LORA, RWKV_GATE_LORA = 64, 64, 128
MLSTM_CONV = 4
D_FF = 2816
ROPE_BASE = 10000.0
RMS_EPS = 1e-6
GN_EPS = 1e-5

RET_IN = 4 * RET_W
RWKV_IN = 3 * RWKV_W + RWKV_DECAY_LORA + RWKV_ICL_LORA + RWKV_GATE_LORA
MLSTM_IN = 4 * MLSTM_W + 2 * MLSTM_HEADS
D_IN = RET_IN + RWKV_IN + MLSTM_IN
TILE = 128
SUB = 8
MLSTM_WP = MLSTM_HEADS * TILE
MLSTM_IN_PAD = 4 * MLSTM_WP + TILE
D_IN_PAD = RET_IN + RWKV_IN + MLSTM_IN_PAD
D_MIX_PAD = RET_W + RWKV_W + MLSTM_WP
RWKV_OFF = RET_IN
MLSTM_OFF = RET_IN + RWKV_IN
LORA_W = RWKV_DECAY_LORA + RWKV_ICL_LORA
FF_CHUNK = 256
IN_PIECE = 512
OUT_PIECE = 256
MIX_BB = 4

WAVE_DELAY = 7
VMEM_LIMIT = 60000 * 1024
_DONE = object()


def _bf(x):
    return x.astype(BF16)


def _mm(a, b):
    return jnp.dot(_bf(a), _bf(b), preferred_element_type=F32)


def _mm_nt(a, b):
    return lax.dot_general(_bf(a), _bf(b), (((1,), (1,)), ((), ())), preferred_element_type=F32)


def _mm_tn(a, b):
    return lax.dot_general(_bf(a), _bf(b), (((0,), (0,)), ((), ())), preferred_element_type=F32)


def _tiles(x):
    return [x[:, t * TILE:(t + 1) * TILE] for t in range(x.shape[1] // TILE)]


def _cumsum_rows(x):
    row = lax.broadcasted_iota(jnp.int32, (x.shape[0], 1), 0)
    k = 1
    while k < x.shape[0]:
        x = x + jnp.where(row >= k, pltpu.roll(x, k, 0), 0.0)
        k *= 2
    return x


def _segsum64(x, lo_half):
    outs = []
    for t in _tiles(x):
        lo = jnp.sum(jnp.where(lo_half, t, 0.0), axis=1, keepdims=True)
        hi = jnp.sum(jnp.where(lo_half, 0.0, t), axis=1, keepdims=True)
        outs.append(jnp.where(lo_half, lo, hi))
    return jnp.concatenate(outs, axis=1)


def _bd2(y, lo_half):
    yb = _bf(y)
    zero = jnp.zeros_like(yb)
    return jnp.concatenate([jnp.where(lo_half, yb, zero), jnp.where(lo_half, zero, yb)], axis=0)


def _log_sigmoid(x):
    return jnp.minimum(x, 0.0) - jnp.log(1.0 + jnp.exp(-jnp.abs(x)))


def _silu(x):
    return x * jax.nn.sigmoid(x)


def _rms(x):
    return x * lax.rsqrt(jnp.mean(x * x, axis=-1, keepdims=True) + RMS_EPS)


def _shift_rows(prev8, cur, k):
    rolled = pltpu.roll(cur, k, 0)
    row8 = lax.broadcasted_iota(jnp.int32, (SUB, 1), 0)
    first = jnp.where(row8 < k, pltpu.roll(prev8, k, 0), rolled[0:SUB])
    return jnp.concatenate([first, rolled[SUB:]], axis=0)


def _mod_kernel(c_ref, w_ref, b_ref, o_ref):
    cond = _silu(c_ref[...])
    o_ref[0] = _mm(cond, w_ref[0]) + b_ref[0]


def _modulation(c, ada_w, ada_b):
    depth, d, n = ada_w.shape
    b = c.shape[0]
    tn = 1536
    return pl.pallas_call(
        _mod_kernel,
        grid=(depth, n // tn),
        in_specs=[
            pl.BlockSpec((b, d), lambda l, j: (0, 0)),
            pl.BlockSpec((1, d, tn), lambda l, j: (l, 0, j)),
            pl.BlockSpec((1, 1, tn), lambda l, j: (l, 0, j)),
        ],
        out_specs=pl.BlockSpec((1, b, tn), lambda l, j: (l, 0, j)),
        out_shape=jax.ShapeDtypeStruct((depth, b, n), F32),
        compiler_params=pltpu.CompilerParams(dimension_semantics=("arbitrary", "arbitrary"),
                                             vmem_limit_bytes=VMEM_LIMIT),
        name="adaln_modulation",
    )(c, ada_w, ada_b.reshape(depth, 1, n))


def _retention_chunk(cols, write, cos, sin, dec_ref, cdec, mask2f, lo_half, gain, state_ref):
    w = RET_W
    lane = lax.broadcasted_iota(jnp.int32, (CHUNK, w), 1)
    first_half = (lane & (RET_HD - 1)) < RET_HD // 2

    def rope(z):
        swapped = jnp.where(first_half, pltpu.roll(z, w - RET_HD // 2, 1), pltpu.roll(z, RET_HD // 2, 1))
        return z * cos + swapped * sin

    q = rope(cols(0, w))
    k = rope(cols(w, 2 * w)) * (RET_HD ** -0.5)
    v = cols(2 * w, 3 * w)
    g = cols(3 * w, 4 * w)
    intra, q_dec = _tiles(dec_ref[0]), _tiles(dec_ref[1])
    q_t, k_t, v_t, kd_t = _tiles(q), _tiles(k), _tiles(v), _tiles(k * dec_ref[2])
    nt = len(q_t)
    states = [state_ref[t] for t in range(nt)]
    scores = [_mm_nt(q_t[t], _bd2(k_t[t], lo_half)) * intra[t] for t in range(nt)]
    inter = [_mm(q_t[t], states[t]) * q_dec[t] for t in range(nt)]
    yield
    y = jnp.concatenate([_mm(scores[t], _bd2(v_t[t], lo_half)) + inter[t] for t in range(nt)], axis=1)
    for t in range(nt):
        state_ref[t] = states[t] * cdec[:, t * TILE:(t + 1) * TILE] + mask2f * _mm_tn(kd_t[t], v_t[t])
    yield
    yc = y - _segsum64(y, lo_half) * (1.0 / RET_HD)
    var = _segsum64(yc * yc, lo_half) * (1.0 / RET_HD)
    write(yc * lax.rsqrt(var + GN_EPS) * gain * _silu(g))


def _rwkv_chunk(cols, write, tril, stril, mask2f, lo_half, mu, p384, w2a2, g2, state_ref, prev_ref):
    w = RWKV_W
    x = cols(RWKV_OFF, RWKV_OFF + RWKV_IN)
    prev = _shift_rows(prev_ref[...], x, 1)
    prev_ref[...] = x[CHUNK - SUB:CHUNK]
    xs = x + (prev - x) * mu
    r = xs[:, 0:w]
    k = xs[:, w:2 * w]
    v = xs[:, 2 * w:3 * w]
    lora = xs[:, 3 * w:3 * w + LORA_W]
    gl = xs[:, 3 * w + LORA_W:RWKV_IN]
    w0, a0, k_k, k_a, r_k, gain = (p384[i:i + 1, :] for i in range(6))

    lane = lax.broadcasted_iota(jnp.int32, (CHUNK, LORA_W), 1)
    wa = _mm(jnp.where(lane < RWKV_DECAY_LORA, jnp.tanh(lora), lora), w2a2)
    g = _mm(jax.nn.sigmoid(gl), g2)
    kk = k * k_k
    kk_sq = _segsum64(kk * kk, lo_half)
    yield
    log_decay = -jnp.exp(_log_sigmoid(w0 + wa[:, 0:w]) - 0.5)
    a = jax.nn.sigmoid(a0 + wa[:, w:2 * w])
    kk = kk / jnp.maximum(jnp.sqrt(kk_sq), 1e-12)
    k = k * (1.0 + (a - 1.0) * k_a)
    bonus = _segsum64(r * k * r_k, lo_half) * v
    c = _cumsum_rows(log_decay)
    yield
    c_last = c[CHUNK - 1:CHUNK, :]
    p_inv = jnp.exp(-c)
    p_end = jnp.exp(c_last - c)
    p_last = jnp.exp(c_last)
    rh = r * jnp.exp(c)
    bh = -kk * jnp.exp(c - log_decay)
    ka = kk * a
    lhs_t = _tiles(jnp.concatenate([bh, rh], axis=0))
    upd_t = _tiles(jnp.concatenate([ka * p_end, k * p_end], axis=0))
    ah_t, kh_t, v_t = _tiles(ka * p_inv), _tiles(k * p_inv), _tiles(v)
    nt = len(v_t)
    tiles = range(nt)

    states = [state_ref[t] for t in tiles]
    s_all = [_mm_nt(lhs_t[t], jnp.concatenate([_bd2(ah_t[t], lo_half), _bd2(kh_t[t], lo_half), _bf(states[t])],
                                              axis=0)) for t in tiles]
    s_a = [z[:, 0:TILE] for z in s_all]
    from_state = [z[:, 2 * TILE:3 * TILE] for z in s_all]
    bd_v = [_bd2(v_t[t], lo_half) for t in tiles]
    tri2 = jnp.concatenate([stril, tril], axis=0)
    yield
    uy = [from_state[t] + _mm(s_all[t][:, TILE:2 * TILE] * tri2, bd_v[t]) for t in tiles]
    u = [z[0:CHUNK] for z in uy]
    y_part = [z[CHUNK:] for z in uy]
    n_pow = [s_a[t][0:CHUNK] * stril for t in tiles]
    yield
    steps = CHUNK.bit_length() - 1
    for j in range(steps):
        if j + 1 < steps:
            both = [_mm(n_pow[t], jnp.concatenate([_bd2(u[t], lo_half), _bd2(n_pow[t], lo_half)], axis=1))
                    for t in tiles]
            u = [u[t] + both[t][:, 0:TILE] for t in tiles]
            n_pow = [both[t][:, TILE:2 * TILE] for t in tiles]
        else:
            u = [u[t] + _mm(n_pow[t], _bd2(u[t], lo_half)) for t in tiles]
        yield
    y = jnp.concatenate([y_part[t] + _mm(s_a[t][CHUNK:] * tril, _bd2(u[t], lo_half)) for t in tiles], axis=1)
    for t in tiles:
        upd = _mm_tn(jnp.concatenate([u[t], v_t[t]], axis=0), upd_t[t])
        state_ref[t] = states[t] * p_last[:, t * TILE:(t + 1) * TILE] + mask2f * upd
    yield
    yc = y - _segsum64(y, lo_half) * (1.0 / RWKV_HD)
    var = _segsum64(yc * yc, lo_half) * (1.0 / RWKV_HD)
    write((yc * lax.rsqrt(var + GN_EPS) * gain + bonus) * g)


def _mlstm_chunk(cols, write, tril, stril, lo_half, convw, convb, gbias, gain, c_ref, n_ref, m_ref, prev_ref):
    wp = MLSTM_WP
    nh = MLSTM_HEADS
    lane_g = lax.broadcasted_iota(jnp.int32, (1, TILE), 1)

    def col(x, h):
        return jnp.sum(jnp.where(lane_g == h, x, 0.0), axis=1, keepdims=True)

    def spread_s(cs):
        return jnp.concatenate([jnp.where(lo_half, cs[2 * p], cs[2 * p + 1]) for p in range(nh // 2)], axis=1)

    def spread_w(cs):
        return jnp.concatenate([jnp.broadcast_to(c, (c.shape[0], TILE)) for c in cs], axis=1)

    off = MLSTM_OFF
    qk_pre = cols(off, off + 2 * wp)
    prev8 = prev_ref[...]
    prev_ref[...] = qk_pre[CHUNK - SUB:CHUNK]
    conv = convb + qk_pre * convw[MLSTM_CONV - 1:MLSTM_CONV, :]
    for j in range(MLSTM_CONV - 1):
        conv = conv + _shift_rows(prev8, qk_pre, MLSTM_CONV - 1 - j) * convw[j:j + 1, :]
    qk = _silu(conv)
    q = qk[:, 0:wp]
    k = qk[:, wp:2 * wp] * (MLSTM_HD ** -0.5)
    v = cols(off + 2 * wp, off + 3 * wp)
    o = cols(off + 3 * wp, off + 4 * wp)
    gates = cols(off + 4 * wp, off + 4 * wp + TILE) + gbias
    i_pre = gates
    log_f = _log_sigmoid(pltpu.roll(gates, TILE - nh, 1))
    bcum = _cumsum_rows(log_f)
    yield
    m_prev = m_ref[0:1, :]

    tril2 = jnp.concatenate([tril] * (nh // 2), axis=1)
    eye2 = tril2 - jnp.concatenate([stril] * (nh // 2), axis=1)
    bcum_c = [col(bcum, h) for h in range(nh)]
    src_c = [col(i_pre - bcum, h) for h in range(nh)]
    src_row = jnp.sum(spread_s(src_c) * eye2, axis=0, keepdims=True)
    d_mat = jnp.where(tril2 > 0.0, spread_s(bcum_c) + src_row, -jnp.inf)
    a_c = [bcum_c[h] + col(m_prev, h) for h in range(nh)]
    head_lanes = [lo_half, jnp.logical_not(lo_half)]
    d_tiles = _tiles(d_mat)
    m_c = [jnp.maximum(a_c[h], jnp.max(jnp.where(head_lanes[h % 2], d_tiles[h // 2], -jnp.inf),
                                       axis=1, keepdims=True)) for h in range(nh)]
    inter_c = [jnp.exp(a_c[h] - m_c[h]) for h in range(nh)]
    weights = jnp.exp(d_mat - spread_s(m_c))

    q_t, k_t, v_t = _tiles(q), _tiles(k), _tiles(v)
    zero = jnp.zeros((CHUNK, TILE), BF16)

    def pair_bd(z_t, p):
        a, b = _bf(z_t[2 * p]), _bf(z_t[2 * p + 1])
        return jnp.concatenate([jnp.concatenate([a, zero], axis=1), jnp.concatenate([zero, b], axis=1)], axis=0)

    pairs = range(nh // 2)
    heads = range(nh)
    c_states = [c_ref[h] for h in heads]
    raw = [_mm_nt(jnp.concatenate([q_t[2 * p], q_t[2 * p + 1]], axis=1), pair_bd(k_t, p)) for p in pairs]
    from_state = [_mm(q_t[h], c_states[h]) for h in heads]
    yield
    s_tiles = [raw[p] * weights[:, p * TILE:(p + 1) * TILE] for p in pairs]
    num_tiles = [z for p in pairs for z in _tiles(_mm(s_tiles[p], pair_bd(v_t, p)))]
    n_state = n_ref[0:1, :]
    qn_t = _tiles(q * n_state)
    b_last = bcum[CHUNK - 1:CHUNK, :]
    g_s = b_last - bcum + i_pre
    m_new = jnp.maximum(b_last + m_prev, jnp.max(g_s, axis=0, keepdims=True))
    ws = jnp.exp(g_s - m_new)
    carry = jnp.exp(b_last + m_prev - m_new)
    kw_t = [k_t[h] * col(ws, h) for h in heads]
    for h in heads:
        c_ref[h] = c_states[h] * col(carry, h) + _mm_tn(kw_t[h], v_t[h])
    n_ref[0:1, :] = (n_state * spread_w([col(carry, h) for h in heads])
                     + jnp.sum(jnp.concatenate(kw_t, axis=1), axis=0, keepdims=True))
    m_ref[0:1, :] = m_new
    yield
    pad_mask = jnp.where(lane_g < MLSTM_HD, 1.0, 0.0)
    hs = []
    for h in heads:
        num = num_tiles[h] + inter_c[h] * from_state[h]
        den = (jnp.sum(jnp.where(head_lanes[h % 2], s_tiles[h // 2], 0.0), axis=1, keepdims=True)
               + inter_c[h] * jnp.sum(qn_t[h], axis=1, keepdims=True))
        hc = num / jnp.maximum(jnp.abs(den), jnp.exp(-m_c[h]))
        mean = jnp.sum(hc, axis=1, keepdims=True) * (1.0 / MLSTM_HD)
        yc = (hc - mean) * pad_mask
        var = jnp.sum(yc * yc, axis=1, keepdims=True) * (1.0 / MLSTM_HD)
        hs.append(yc * lax.rsqrt(var + GN_EPS))
    write(jnp.concatenate(hs, axis=1) * gain * jax.nn.sigmoid(o))


def _mixer_constants(seq):
    h = jnp.arange(RET_HEADS, dtype=F32)
    log_g = jnp.log1p(-jnp.exp2(-5.0 - h))
    idx = jnp.arange(CHUNK, dtype=F32)
    diff = idx[:, None] - idx[None, :]
    intra = jnp.where(diff >= 0, jnp.exp(log_g[:, None, None] * jnp.maximum(diff, 0.0)), 0.0)
    intra = jnp.transpose(intra, (1, 0, 2)).reshape(CHUNK, RET_HEADS * CHUNK)
    q_decay = jnp.exp(log_g[:, None] * (idx + 1.0))
    k_decay = jnp.exp(log_g[:, None] * (CHUNK - 1.0 - idx))
    rep = lambda z: jnp.repeat(z.T, RET_HD, axis=1)
    dec = jnp.stack([intra, rep(q_decay), rep(k_decay)])
    cdec = jnp.repeat(jnp.exp(log_g * CHUNK), RET_HD)[None, :]

    inv = ROPE_BASE ** (-jnp.arange(0, RET_HD, 2, dtype=F32) / RET_HD)
    ang = jnp.arange(seq, dtype=F32)[:, None] * inv[None, :]
    cos = jnp.tile(jnp.cos(ang), (1, 2 * RET_HEADS))
    sin = jnp.tile(jnp.concatenate([-jnp.sin(ang), jnp.sin(ang)], axis=1), (1, RET_HEADS))

    half = np.arange(TILE) // CHUNK
    mask2 = (half[:, None] == half[None, :]).astype(np.float32)
    t = np.arange(CHUNK)
    tril = (t[:, None] >= t[None, :]).astype(np.float32)
    stril = (t[:, None] > t[None, :]).astype(np.float32)
    tri = np.stack([np.tile(tril, (1, 2)), np.tile(stril, (1, 2))])
    return dict(cos=cos, sin=sin, dec=dec, cdec=cdec, tri=jnp.asarray(tri), mask2f=jnp.asarray(mask2))


def _dense_chain(xin_ref, xres_ref, sh_ref, sc_ref, gm_ref, shf_ref, scf_ref, gf_ref, gin_ref, win_ref,
                 wout_ref, gff_ref, wi_ref, wo_ref, fin_ref, out_ref, proj_slot, mixed_slot, act_scr, x1_scr,
                 final):
    rows = range(MIX_BB)
    h = jnp.concatenate([_bf(_rms(xin_ref[b]) * gin_ref[...] * (1.0 + sc_ref[b]) + sh_ref[b]) for b in rows],
                        axis=0)
    for n0 in range(0, D_IN_PAD, IN_PIECE):
        proj_slot[:, n0:n0 + IN_PIECE] = jnp.dot(h, win_ref[:, n0:n0 + IN_PIECE], preferred_element_type=F32)
        yield
    att = jnp.dot(mixed_slot[...], wout_ref[...], preferred_element_type=F32)
    yield
    hs = []
    for b in rows:
        x1 = xres_ref[b] + gm_ref[b] * att[b * CHUNK:(b + 1) * CHUNK]
        x1_scr[b * CHUNK:(b + 1) * CHUNK, :] = x1
        hs.append(_bf(_rms(x1) * gff_ref[...] * (1.0 + scf_ref[b]) + shf_ref[b]))
    h = jnp.concatenate(hs, axis=0)
    for f0 in range(0, D_FF, FF_CHUNK):
        gate = jnp.dot(h, wi_ref[:, f0:f0 + FF_CHUNK], preferred_element_type=F32)
        up = jnp.dot(h, wi_ref[:, D_FF + f0:D_FF + f0 + FF_CHUNK], preferred_element_type=F32)
        act_scr[:, f0:f0 + FF_CHUNK] = _bf(_silu(gate) * up)
        yield
    pieces = []
    for n0 in range(0, D_MODEL, OUT_PIECE):
        pieces.append(jnp.dot(act_scr[...], wo_ref[:, n0:n0 + OUT_PIECE], preferred_element_type=F32))
        yield
    ffn = jnp.concatenate(pieces, axis=1)
    for b in rows:
        x2 = x1_scr[b * CHUNK:(b + 1) * CHUNK, :] + gf_ref[b] * ffn[b * CHUNK:(b + 1) * CHUNK]
        if final:
            x2 = _rms(x2) * fin_ref[...]
        out_ref[b] = x2


def _layer_kernel(xin_ref, xres_ref, sh_ref, sc_ref, gm_ref, shf_ref, scf_ref, gf_ref, cos_ref, sin_ref,
                  dec_ref, cdec_ref, tri_ref, mask2f_ref,
                  gret_ref, mu_ref, p384_ref, w2a2_ref, g2_ref, convw_ref, convb_ref, gbias_ref, gml_ref,
                  gin_ref, win_ref, wout_ref, gff_ref, wi_ref, wo_ref, fin_ref,
                  out_ref,
                  proj_scr, mixed_scr, act_scr, x1_scr, s_ret, s_rwkv, prev_rwkv, c_ml, n_ml, m_ml, prev_qk,
                  *, n_chunks, n_tiles, final):
    s = pl.program_id(0)
    mix_tile = jnp.clip(s - 1, 0, n_tiles - 1)
    slot_in = lax.rem(s, 2)
    slot_mix = 1 - slot_in

    @pl.when(s == 0)
    def _():
        proj_scr[1] = jnp.zeros(proj_scr.shape[1:], proj_scr.dtype)
        mixed_scr[0] = jnp.zeros(mixed_scr.shape[1:], mixed_scr.dtype)

    @pl.when(lax.rem(mix_tile, n_chunks) == 0)
    def _():
        for ref in (s_ret, s_rwkv, prev_rwkv, c_ml, n_ml, m_ml, prev_qk):
            ref[...] = jnp.zeros(ref.shape, ref.dtype)

    tril, stril = tri_ref[0], tri_ref[1]
    mask2f = mask2f_ref[...]
    lo_half = lax.broadcasted_iota(jnp.int32, (1, TILE), 1) < TILE // 2
    cos = cos_ref[...]
    sin = sin_ref[...]
    proj_mix = proj_scr.at[slot_mix]
    mixed_mix = mixed_scr.at[slot_mix]
    dense = _dense_chain(xin_ref, xres_ref, sh_ref, sc_ref, gm_ref, shf_ref, scf_ref, gf_ref, gin_ref, win_ref,
                         wout_ref, gff_ref, wi_ref, wo_ref, fin_ref, out_ref, proj_scr.at[slot_in],
                         mixed_scr.at[slot_in], act_scr, x1_scr, final)
    mixers = []
    for bb in range(MIX_BB):
        r0 = bb * CHUNK
        cols = lambda lo, hi, r0=r0: proj_mix[r0:r0 + CHUNK, lo:hi]

        def writer(lo, hi, r0=r0):
            def write(y):
                mixed_mix[r0:r0 + CHUNK, lo:hi] = y.astype(mixed_mix.dtype)
            return write

        mixers.append(_rwkv_chunk(cols, writer(RET_W, RET_W + RWKV_W), tril, stril, mask2f, lo_half,
                                  mu_ref[...], p384_ref[...], w2a2_ref[...], g2_ref[...],
                                  s_rwkv.at[bb], prev_rwkv.at[bb]))
        mixers.append(_retention_chunk(cols, writer(0, RET_W), cos, sin, dec_ref, cdec_ref[...], mask2f,
                                       lo_half, gret_ref[...], s_ret.at[bb]))
        mixers.append(_mlstm_chunk(cols, writer(RET_W + RWKV_W, D_MIX_PAD), tril, stril, lo_half,
                                   convw_ref[...], convb_ref[...], gbias_ref[...], gml_ref[...],
                                   c_ml.at[bb], n_ml.at[bb], m_ml.at[bb], prev_qk.at[bb]))
    per_row = len(mixers) // MIX_BB
    first, second = mixers[:per_row * (MIX_BB // 2)], mixers[per_row * (MIX_BB // 2):]
    active = [dense] + first
    rnd = 0
    while active:
        if rnd == WAVE_DELAY:
            active = active + second
        active = [g for g in active if next(g, _DONE) is not _DONE]
        rnd += 1
    assert rnd > WAVE_DELAY


def _layer(x3, mod, consts, layer_params, final_gain, layer, final):
    batch, seq, d = x3.shape
    n_chunks = seq // CHUNK
    n_tiles = (batch // MIX_BB) * n_chunks
    rows = MIX_BB * CHUNK

    def tile(lag):
        def index(s):
            t = jnp.clip(s - lag, 0, n_tiles - 1)
            return t // n_chunks, lax.rem(t, n_chunks)
        return index

    x_spec = lambda lag: pl.BlockSpec((MIX_BB, CHUNK, d), lambda s: (*tile(lag)(s), 0))
    mod_spec = lambda lag, col: pl.BlockSpec((None, MIX_BB, 1, d), lambda s: (layer, tile(lag)(s)[0], 0, col))
    rope_spec = pl.BlockSpec((CHUNK, RET_W), lambda s: (tile(1)(s)[1], 0))
    full = lambda a: pl.BlockSpec(a.shape, lambda s: (0,) * a.ndim, pipeline_mode=pl.Buffered(1))
    of_layer = lambda a: pl.BlockSpec((None,) + a.shape[1:], lambda s: (layer,) + (0,) * (a.ndim - 1),
                                      pipeline_mode=pl.Buffered(1))
    const_inputs = [consts["dec"], consts["cdec"], consts["tri"], consts["mask2f"]]
    return pl.pallas_call(
        functools.partial(_layer_kernel, n_chunks=n_chunks, n_tiles=n_tiles, final=final),
        grid=(n_tiles + 2,),
        in_specs=[x_spec(0), x_spec(2), mod_spec(0, 0), mod_spec(0, 1), mod_spec(2, 2), mod_spec(2, 3),
                  mod_spec(2, 4), mod_spec(2, 5), rope_spec, rope_spec]
                 + [full(a) for a in const_inputs] + [of_layer(a) for a in layer_params] + [full(final_gain)],
        out_specs=x_spec(2),
        out_shape=jax.ShapeDtypeStruct((batch, seq, d), F32),
        scratch_shapes=[
            pltpu.VMEM((2, rows, D_IN_PAD), F32),
            pltpu.VMEM((2, rows, D_MIX_PAD), BF16),
            pltpu.VMEM((rows, D_FF), BF16),
            pltpu.VMEM((rows, d), F32),
            pltpu.VMEM((MIX_BB, RET_W // TILE, TILE, TILE), F32),
            pltpu.VMEM((MIX_BB, RWKV_W // TILE, TILE, TILE), F32),
            pltpu.VMEM((MIX_BB, SUB, RWKV_IN), F32),
            pltpu.VMEM((MIX_BB, MLSTM_HEADS, TILE, TILE), F32),
            pltpu.VMEM((MIX_BB, SUB, MLSTM_WP), F32),
            pltpu.VMEM((MIX_BB, SUB, TILE), F32),
            pltpu.VMEM((MIX_BB, SUB, 2 * MLSTM_WP), F32),
        ],
        compiler_params=pltpu.CompilerParams(dimension_semantics=("arbitrary",),
                                             vmem_limit_bytes=VMEM_LIMIT),
        name="layer",
    )(x3, x3, mod, mod, mod, mod, mod, mod, consts["cos"], consts["sin"],
      *const_inputs, *layer_params, final_gain)


def _pad_heads_index(n_parts):
    idx = -np.ones(n_parts * MLSTM_WP, np.int64)
    for part in range(n_parts):
        for h in range(MLSTM_HEADS):
            dst = part * MLSTM_WP + h * TILE
            idx[dst:dst + MLSTM_HD] = part * MLSTM_W + h * MLSTM_HD + np.arange(MLSTM_HD)
    return idx


def _gather_cols(a, idx):
    return jnp.where(jnp.asarray(idx >= 0), a[..., np.maximum(idx, 0)], 0.0)


def _mixer_params(mix_gn, rwkv_mu, rwkv_w0, rwkv_w2, rwkv_a0, rwkv_a2, rwkv_g2, rwkv_k_k, rwkv_k_a, rwkv_r_k,
                  mlstm_conv_w, mlstm_conv_b, mlstm_i_b, mlstm_f_b):
    depth = mix_gn.shape[0]
    zeros = jnp.zeros((depth, RWKV_W), F32)
    p384 = jnp.stack([rwkv_w0, rwkv_a0, rwkv_k_k, rwkv_k_a, rwkv_r_k.reshape(depth, RWKV_W),
                      mix_gn[:, RET_W:RET_W + RWKV_W], zeros, zeros], axis=1)
    w2a2 = jnp.zeros((depth, LORA_W, 2 * RWKV_W), F32)
    w2a2 = w2a2.at[:, :RWKV_DECAY_LORA, :RWKV_W].set(rwkv_w2).at[:, RWKV_DECAY_LORA:, RWKV_W:].set(rwkv_a2)
    gbias = jnp.zeros((depth, 1, TILE), F32)
    gbias = gbias.at[:, 0, :MLSTM_HEADS].set(mlstm_i_b).at[:, 0, MLSTM_HEADS:2 * MLSTM_HEADS].set(mlstm_f_b)
    qk_idx = _pad_heads_index(2)
    convw = jnp.zeros((depth, SUB, 2 * MLSTM_WP), F32).at[:, :MLSTM_CONV].set(_gather_cols(mlstm_conv_w, qk_idx))
    convb = _gather_cols(mlstm_conv_b, qk_idx)[:, None, :]
    gain_ml = _gather_cols(mix_gn[:, RET_W + RWKV_W:], _pad_heads_index(1))[:, None, :]
    return [mix_gn[:, None, :RET_W], rwkv_mu[:, None, :], p384, _bf(w2a2), _bf(rwkv_g2), convw, convb, gbias,
            gain_ml]


def _pad_w_in(w):
    idx = _pad_heads_index(4)
    gates = -np.ones(TILE, np.int64)
    gates[:2 * MLSTM_HEADS] = 4 * MLSTM_W + np.arange(2 * MLSTM_HEADS)
    ml = _gather_cols(w[..., MLSTM_OFF:], np.concatenate([idx, gates]))
    return jnp.concatenate([w[..., :MLSTM_OFF], ml], axis=-1)


def _pad_w_out(w):
    idx = _pad_heads_index(1)
    ml = jnp.where(jnp.asarray(idx >= 0)[:, None], w[:, RET_W + RWKV_W + np.maximum(idx, 0)], 0.0)
    return jnp.concatenate([w[:, :RET_W + RWKV_W], ml], axis=1)


def kernel(x, c, ada_w, ada_b, norm_mix, norm_ffn, w_in, mix_gn, rwkv_mu, rwkv_w0, rwkv_w2, rwkv_a0, rwkv_a2, rwkv_g2, rwkv_k_k, rwkv_k_a, rwkv_r_k, mlstm_conv_w, mlstm_conv_b, mlstm_i_b, mlstm_f_b, w_out, ffn_w_in, ffn_w_out, final_norm):
    batch, seq, d = x.shape
    depth = ada_w.shape[0]
    assert d == D_MODEL and seq % 256 == 0 and batch % MIX_BB == 0
    consts = _mixer_constants(seq)
    mod = _modulation(c, ada_w, ada_b).reshape(depth, batch, 1, 6 * d)
    mixer_params = _mixer_params(mix_gn, rwkv_mu, rwkv_w0, rwkv_w2, rwkv_a0, rwkv_a2, rwkv_g2, rwkv_k_k, rwkv_k_a,
                                 rwkv_r_k, mlstm_conv_w, mlstm_conv_b, mlstm_i_b, mlstm_f_b)
    dense_params = [norm_mix[:, None, :], _pad_w_in(_bf(w_in)), _pad_w_out(_bf(w_out)), norm_ffn[:, None, :],
                    _bf(ffn_w_in), _bf(ffn_w_out)]
    for l in range(depth):
        x = _layer(x, mod, consts, mixer_params + dense_params, final_norm[None, :], l, final=(l == depth - 1))
    return x
```

```python
import functools

import numpy as np
import jax
import jax.numpy as jnp
from jax import lax
from jax.experimental import pallas as pl
from jax.experimental.pallas import tpu as pltpu

F32 = jnp.float32
BF16 = jnp.bfloat16

D_MODEL = 1024
CHUNK = 64
RET_HEADS, RET_HD = 4, 64
RET_W = RET_HEADS * RET_HD
RWKV_HEADS, RWKV_HD = 6, 64
RWKV_W = RWKV_HEADS * RWKV_HD
MLSTM_HEADS, MLSTM_HD = 4, 96
MLSTM_W = MLSTM_HEADS * MLSTM_HD
D_MIX = RET_W + RWKV_W + MLSTM_W
RWKV_DECAY_LORA, RWKV_ICL_LORA, RWKV_GATE_LORA = 64, 64, 128
MLSTM_CONV = 4
D_FF = 2816
ROPE_BASE = 10000.0
RMS_EPS = 1e-6
GN_EPS = 1e-5

RET_IN = 4 * RET_W
RWKV_IN = 3 * RWKV_W + RWKV_DECAY_LORA + RWKV_ICL_LORA + RWKV_GATE_LORA
MLSTM_IN = 4 * MLSTM_W + 2 * MLSTM_HEADS
D_IN = RET_IN + RWKV_IN + MLSTM_IN
TILE = 128
SUB = 8
MLSTM_WP = MLSTM_HEADS * TILE
MLSTM_IN_PAD = 4 * MLSTM_WP + TILE
D_IN_PAD = RET_IN + RWKV_IN + MLSTM_IN_PAD
D_MIX_PAD = RET_W + RWKV_W + MLSTM_WP
RWKV_OFF = RET_IN
MLSTM_OFF = RET_IN + RWKV_IN
LORA_W = RWKV_DECAY_LORA + RWKV_ICL_LORA
FF_CHUNK = 256
IN_PIECE = 512
OUT_PIECE = 256
MIX_BB = 4

WAVE_DELAY = 7
VMEM_LIMIT = 60000 * 1024
_DONE = object()


def _bf(x):
    return x.astype(BF16)


def _mm(a, b):
    return jnp.dot(_bf(a), _bf(b), preferred_element_type=F32)


def _mm_nt(a, b):
    return lax.dot_general(_bf(a), _bf(b), (((1,), (1,)), ((), ())), preferred_element_type=F32)


def _mm_tn(a, b):
    return lax.dot_general(_bf(a), _bf(b), (((0,), (0,)), ((), ())), preferred_element_type=F32)


def _tiles(x):
    return [x[:, t * TILE:(t + 1) * TILE] for t in range(x.shape[1] // TILE)]


def _cumsum_rows(x):
    row = lax.broadcasted_iota(jnp.int32, (x.shape[0], 1), 0)
    k = 1
    while k < x.shape[0]:
        x = x + jnp.where(row >= k, pltpu.roll(x, k, 0), 0.0)
        k *= 2
    return x


def _segsum64(x, lo_half):
    outs = []
    for t in _tiles(x):
        lo = jnp.sum(jnp.where(lo_half, t, 0.0), axis=1, keepdims=True)
        hi = jnp.sum(jnp.where(lo_half, 0.0, t), axis=1, keepdims=True)
        outs.append(jnp.where(lo_half, lo, hi))
    return jnp.concatenate(outs, axis=1)


def _bd2(y, lo_half):
    yb = _bf(y)
    zero = jnp.zeros_like(yb)
    return jnp.concatenate([jnp.where(lo_half, yb, zero), jnp.where(lo_half, zero, yb)], axis=0)


def _log_sigmoid(x):
    return jnp.minimum(x, 0.0) - jnp.log(1.0 + jnp.exp(-jnp.abs(x)))


def _silu(x):
    return x * jax.nn.sigmoid(x)


def _rms(x):
    return x * lax.rsqrt(jnp.mean(x * x, axis=-1, keepdims=True) + RMS_EPS)


def _shift_rows(prev8, cur, k):
    rolled = pltpu.roll(cur, k, 0)
    row8 = lax.broadcasted_iota(jnp.int32, (SUB, 1), 0)
    first = jnp.where(row8 < k, pltpu.roll(prev8, k, 0), rolled[0:SUB])
    return jnp.concatenate([first, rolled[SUB:]], axis=0)


def _mod_kernel(c_ref, w_ref, b_ref, o_ref):
    cond = _silu(c_ref[...])
    o_ref[0] = _mm(cond, w_ref[0]) + b_ref[0]


def _modulation(c, ada_w, ada_b):
    depth, d, n = ada_w.shape
    b = c.shape[0]
    tn = 1536
    return pl.pallas_call(
        _mod_kernel,
        grid=(depth, n // tn),
        in_specs=[
            pl.BlockSpec((b, d), lambda l, j: (0, 0)),
            pl.BlockSpec((1, d, tn), lambda l, j: (l, 0, j)),
            pl.BlockSpec((1, 1, tn), lambda l, j: (l, 0, j)),
        ],
        out_specs=pl.BlockSpec((1, b, tn), lambda l, j: (l, 0, j)),
        out_shape=jax.ShapeDtypeStruct((depth, b, n), F32),
        compiler_params=pltpu.CompilerParams(dimension_semantics=("arbitrary", "arbitrary"),
                                             vmem_limit_bytes=VMEM_LIMIT),
        name="adaln_modulation",
    )(c, ada_w, ada_b.reshape(depth, 1, n))


def _retention_chunk(cols, write, cos, sin, dec_ref, cdec, mask2f, lo_half, gain, state_ref):
    w = RET_W
    lane = lax.broadcasted_iota(jnp.int32, (CHUNK, w), 1)
    first_half = (lane & (RET_HD - 1)) < RET_HD // 2

    def rope(z):
        swapped = jnp.where(first_half, pltpu.roll(z, w - RET_HD // 2, 1), pltpu.roll(z, RET_HD // 2, 1))
        return z * cos + swapped * sin

    q = rope(cols(0, w))
    k = rope(cols(w, 2 * w)) * (RET_HD ** -0.5)
    v = cols(2 * w, 3 * w)
    g = cols(3 * w, 4 * w)
    intra, q_dec = _tiles(dec_ref[0]), _tiles(dec_ref[1])
    q_t, k_t, v_t, kd_t = _tiles(q), _tiles(k), _tiles(v), _tiles(k * dec_ref[2])
    nt = len(q_t)
    states = [state_ref[t] for t in range(nt)]
    scores = [_mm_nt(q_t[t], _bd2(k_t[t], lo_half)) * intra[t] for t in range(nt)]
    inter = [_mm(q_t[t], states[t]) * q_dec[t] for t in range(nt)]
    yield
    y = jnp.concatenate([_mm(scores[t], _bd2(v_t[t], lo_half)) + inter[t] for t in range(nt)], axis=1)
    for t in range(nt):
        state_ref[t] = states[t] * cdec[:, t * TILE:(t + 1) * TILE] + mask2f * _mm_tn(kd_t[t], v_t[t])
    yield
    yc = y - _segsum64(y, lo_half) * (1.0 / RET_HD)
    var = _segsum64(yc * yc, lo_half) * (1.0 / RET_HD)
    write(yc * lax.rsqrt(var + GN_EPS) * gain * _silu(g))


def _rwkv_chunk(cols, write, tril, stril, mask2f, lo_half, mu, p384, w2a2, g2, state_ref, prev_ref):
    w = RWKV_W
    x = cols(RWKV_OFF, RWKV_OFF + RWKV_IN)
    prev = _shift_rows(prev_ref[...], x, 1)
    prev_ref[...] = x[CHUNK - SUB:CHUNK]
    xs = x + (prev - x) * mu
    r = xs[:, 0:w]
    k = xs[:, w:2 * w]
    v = xs[:, 2 * w:3 * w]
    lora = xs[:, 3 * w:3 * w + LORA_W]
    gl = xs[:, 3 * w + LORA_W:RWKV_IN]
    w0, a0, k_k, k_a, r_k, gain = (p384[i:i + 1, :] for i in range(6))

    lane = lax.broadcasted_iota(jnp.int32, (CHUNK, LORA_W), 1)
    wa = _mm(jnp.where(lane < RWKV_DECAY_LORA, jnp.tanh(lora), lora), w2a2)
    g = _mm(jax.nn.sigmoid(gl), g2)
    kk = k * k_k
    kk_sq = _segsum64(kk * kk, lo_half)
    yield
    log_decay = -jnp.exp(_log_sigmoid(w0 + wa[:, 0:w]) - 0.5)
    a = jax.nn.sigmoid(a0 + wa[:, w:2 * w])
    kk = kk / jnp.maximum(jnp.sqrt(kk_sq), 1e-12)
    k = k * (1.0 + (a - 1.0) * k_a)
    bonus = _segsum64(r * k * r_k, lo_half) * v
    c = _cumsum_rows(log_decay)
    yield
    c_last = c[CHUNK - 1:CHUNK, :]
    p_inv = jnp.exp(-c)
    p_end = jnp.exp(c_last - c)
    p_last = jnp.exp(c_last)
    rh = r * jnp.exp(c)
    bh = -kk * jnp.exp(c - log_decay)
    ka = kk * a
    lhs_t = _tiles(jnp.concatenate([bh, rh], axis=0))
    upd_t = _tiles(jnp.concatenate([ka * p_end, k * p_end], axis=0))
    ah_t, kh_t, v_t = _tiles(ka * p_inv), _tiles(k * p_inv), _tiles(v)
    nt = len(v_t)
    tiles = range(nt)

    states = [state_ref[t] for t in tiles]
    s_all = [_mm_nt(lhs_t[t], jnp.concatenate([_bd2(ah_t[t], lo_half), _bd2(kh_t[t], lo_half), _bf(states[t])],
                                              axis=0)) for t in tiles]
    s_a = [z[:, 0:TILE] for z in s_all]
    from_state = [z[:, 2 * TILE:3 * TILE] for z in s_all]
    bd_v = [_bd2(v_t[t], lo_half) for t in tiles]
    tri2 = jnp.concatenate([stril, tril], axis=0)
    yield
    uy = [from_state[t] + _mm(s_all[t][:, TILE:2 * TILE] * tri2, bd_v[t]) for t in tiles]
    u = [z[0:CHUNK] for z in uy]
    y_part = [z[CHUNK:] for z in uy]
    n_pow = [s_a[t][0:CHUNK] * stril for t in tiles]
    yield
    steps = CHUNK.bit_length() - 1
    for j in range(steps):
        if j + 1 < steps:
            both = [_mm(n_pow[t], jnp.concatenate([_bd2(u[t], lo_half), _bd2(n_pow[t], lo_half)], axis=1))
                    for t in tiles]
            u = [u[t] + both[t][:, 0:TILE] for t in tiles]
            n_pow = [both[t][:, TILE:2 * TILE] for t in tiles]
        else:
            u = [u[t] + _mm(n_pow[t], _bd2(u[t], lo_half)) for t in tiles]
        yield
    y = jnp.concatenate([y_part[t] + _mm(s_a[t][CHUNK:] * tril, _bd2(u[t], lo_half)) for t in tiles], axis=1)
    for t in tiles:
        upd = _mm_tn(jnp.concatenate([u[t], v_t[t]], axis=0), upd_t[t])
        state_ref[t] = states[t] * p_last[:, t * TILE:(t + 1) * TILE] + mask2f * upd
    yield
    yc = y - _segsum64(y, lo_half) * (1.0 / RWKV_HD)
    var = _segsum64(yc * yc, lo_half) * (1.0 / RWKV_HD)
    write((yc * lax.rsqrt(var + GN_EPS) * gain + bonus) * g)


def _mlstm_chunk(cols, write, tril, stril, lo_half, convw, convb, gbias, gain, c_ref, n_ref, m_ref, prev_ref):
    wp = MLSTM_WP
    nh = MLSTM_HEADS
    lane_g = lax.broadcasted_iota(jnp.int32, (1, TILE), 1)

    def col(x, h):
        return jnp.sum(jnp.where(lane_g == h, x, 0.0), axis=1, keepdims=True)

    def spread_s(cs):
        return jnp.concatenate([jnp.where(lo_half, cs[2 * p], cs[2 * p + 1]) for p in range(nh // 2)], axis=1)

    def spread_w(cs):
        return jnp.concatenate([jnp.broadcast_to(c, (c.shape[0], TILE)) for c in cs], axis=1)

    off = MLSTM_OFF
    qk_pre = cols(off, off + 2 * wp)
    prev8 = prev_ref[...]
    prev_ref[...] = qk_pre[CHUNK - SUB:CHUNK]
    conv = convb + qk_pre * convw[MLSTM_CONV - 1:MLSTM_CONV, :]
    for j in range(MLSTM_CONV - 1):
        conv = conv + _shift_rows(prev8, qk_pre, MLSTM_CONV - 1 - j) * convw[j:j + 1, :]
    qk = _silu(conv)
    q = qk[:, 0:wp]
    k = qk[:, wp:2 * wp] * (MLSTM_HD ** -0.5)
    v = cols(off + 2 * wp, off + 3 * wp)
    o = cols(off + 3 * wp, off + 4 * wp)
    gates = cols(off + 4 * wp, off + 4 * wp + TILE) + gbias
    i_pre = gates
    log_f = _log_sigmoid(pltpu.roll(gates, TILE - nh, 1))
    bcum = _cumsum_rows(log_f)
    yield
    m_prev = m_ref[0:1, :]

    tril2 = jnp.concatenate([tril] * (nh // 2), axis=1)
    eye2 = tril2 - jnp.concatenate([stril] * (nh // 2), axis=1)
    bcum_c = [col(bcum, h) for h in range(nh)]
    src_c = [col(i_pre - bcum, h) for h in range(nh)]
    src_row = jnp.sum(spread_s(src_c) * eye2, axis=0, keepdims=True)
    d_mat = jnp.where(tril2 > 0.0, spread_s(bcum_c) + src_row, -jnp.inf)
    a_c = [bcum_c[h] + col(m_prev, h) for h in range(nh)]
    head_lanes = [lo_half, jnp.logical_not(lo_half)]
    d_tiles = _tiles(d_mat)
    m_c = [jnp.maximum(a_c[h], jnp.max(jnp.where(head_lanes[h % 2], d_tiles[h // 2], -jnp.inf),
                                       axis=1, keepdims=True)) for h in range(nh)]
    inter_c = [jnp.exp(a_c[h] - m_c[h]) for h in range(nh)]
    weights = jnp.exp(d_mat - spread_s(m_c))

    q_t, k_t, v_t = _tiles(q), _tiles(k), _tiles(v)
    zero = jnp.zeros((CHUNK, TILE), BF16)

    def pair_bd(z_t, p):
        a, b = _bf(z_t[2 * p]), _bf(z_t[2 * p + 1])
        return jnp.concatenate([jnp.concatenate([a, zero], axis=1), jnp.concatenate([zero, b], axis=1)], axis=0)

    pairs = range(nh // 2)
    heads = range(nh)
    c_states = [c_ref[h] for h in heads]
    raw = [_mm_nt(jnp.concatenate([q_t[2 * p], q_t[2 * p + 1]], axis=1), pair_bd(k_t, p)) for p in pairs]
    from_state = [_mm(q_t[h], c_states[h]) for h in heads]
    yield
    s_tiles = [raw[p] * weights[:, p * TILE:(p + 1) * TILE] for p in pairs]
    num_tiles = [z for p in pairs for z in _tiles(_mm(s_tiles[p], pair_bd(v_t, p)))]
    n_state = n_ref[0:1, :]
    qn_t = _tiles(q * n_state)
    b_last = bcum[CHUNK - 1:CHUNK, :]
    g_s = b_last - bcum + i_pre
    m_new = jnp.maximum(b_last + m_prev, jnp.max(g_s, axis=0, keepdims=True))
    ws = jnp.exp(g_s - m_new)
    carry = jnp.exp(b_last + m_prev - m_new)
    kw_t = [k_t[h] * col(ws, h) for h in heads]
    for h in heads:
        c_ref[h] = c_states[h] * col(carry, h) + _mm_tn(kw_t[h], v_t[h])
    n_ref[0:1, :] = (n_state * spread_w([col(carry, h) for h in heads])
                     + jnp.sum(jnp.concatenate(kw_t, axis=1), axis=0, keepdims=True))
    m_ref[0:1, :] = m_new
    yield
    pad_mask = jnp.where(lane_g < MLSTM_HD, 1.0, 0.0)
    hs = []
    for h in heads:
        num = num_tiles[h] + inter_c[h] * from_state[h]
        den = (jnp.sum(jnp.where(head_lanes[h % 2], s_tiles[h // 2], 0.0), axis=1, keepdims=True)
               + inter_c[h] * jnp.sum(qn_t[h], axis=1, keepdims=True))
        hc = num / jnp.maximum(jnp.abs(den), jnp.exp(-m_c[h]))
        mean = jnp.sum(hc, axis=1, keepdims=True) * (1.0 / MLSTM_HD)
        yc = (hc - mean) * pad_mask
        var = jnp.sum(yc * yc, axis=1, keepdims=True) * (1.0 / MLSTM_HD)
        hs.append(yc * lax.rsqrt(var + GN_EPS))
    write(jnp.concatenate(hs, axis=1) * gain * jax.nn.sigmoid(o))


def _mixer_constants(seq):
    h = jnp.arange(RET_HEADS, dtype=F32)
    log_g = jnp.log1p(-jnp.exp2(-5.0 - h))
    idx = jnp.arange(CHUNK, dtype=F32)
    diff = idx[:, None] - idx[None, :]
    intra = jnp.where(diff >= 0, jnp.exp(log_g[:, None, None] * jnp.maximum(diff, 0.0)), 0.0)
    intra = jnp.transpose(intra, (1, 0, 2)).reshape(CHUNK, RET_HEADS * CHUNK)
    q_decay = jnp.exp(log_g[:, None] * (idx + 1.0))
    k_decay = jnp.exp(log_g[:, None] * (CHUNK - 1.0 - idx))
    rep = lambda z: jnp.repeat(z.T, RET_HD, axis=1)
    dec = jnp.stack([intra, rep(q_decay), rep(k_decay)])
    cdec = jnp.repeat(jnp.exp(log_g * CHUNK), RET_HD)[None, :]

    inv = ROPE_BASE ** (-jnp.arange(0, RET_HD, 2, dtype=F32) / RET_HD)
    ang = jnp.arange(seq, dtype=F32)[:, None] * inv[None, :]
    cos = jnp.tile(jnp.cos(ang), (1, 2 * RET_HEADS))
    sin = jnp.tile(jnp.concatenate([-jnp.sin(ang), jnp.sin(ang)], axis=1), (1, RET_HEADS))

    half = np.arange(TILE) // CHUNK
    mask2 = (half[:, None] == half[None, :]).astype(np.float32)
    t = np.arange(CHUNK)
    tril = (t[:, None] >= t[None, :]).astype(np.float32)
    stril = (t[:, None] > t[None, :]).astype(np.float32)
    tri = np.stack([np.tile(tril, (1, 2)), np.tile(stril, (1, 2))])
    return dict(cos=cos, sin=sin, dec=dec, cdec=cdec, tri=jnp.asarray(tri), mask2f=jnp.asarray(mask2))


def _dense_chain(xin_ref, xres_ref, sh_ref, sc_ref, gm_ref, shf_ref, scf_ref, gf_ref, gin_ref, win_ref,
                 wout_ref, gff_ref, wi_ref, wo_ref, fin_ref, out_ref, proj_slot, mixed_slot, act_scr, x1_scr,
                 final):
    rows = range(MIX_BB)
    h = jnp.concatenate([_bf(_rms(xin_ref[b]) * gin_ref[...] * (1.0 + sc_ref[b]) + sh_ref[b]) for b in rows],
                        axis=0)
    for n0 in range(0, D_IN_PAD, IN_PIECE):
        proj_slot[:, n0:n0 + IN_PIECE] = jnp.dot(h, win_ref[:, n0:n0 + IN_PIECE], preferred_element_type=F32)
        yield
    att = jnp.dot(mixed_slot[...], wout_ref[...], preferred_element_type=F32)
    yield
    hs = []
    for b in rows:
        x1 = xres_ref[b] + gm_ref[b] * att[b * CHUNK:(b + 1) * CHUNK]
        x1_scr[b * CHUNK:(b + 1) * CHUNK, :] = x1
        hs.append(_bf(_rms(x1) * gff_ref[...] * (1.0 + scf_ref[b]) + shf_ref[b]))
    h = jnp.concatenate(hs, axis=0)
    for f0 in range(0, D_FF, FF_CHUNK):
        gate = jnp.dot(h, wi_ref[:, f0:f0 + FF_CHUNK], preferred_element_type=F32)
        up = jnp.dot(h, wi_ref[:, D_FF + f0:D_FF + f0 + FF_CHUNK], preferred_element_type=F32)
        act_scr[:, f0:f0 + FF_CHUNK] = _bf(_silu(gate) * up)
        yield
    pieces = []
    for n0 in range(0, D_MODEL, OUT_PIECE):
        pieces.append(jnp.dot(act_scr[...], wo_ref[:, n0:n0 + OUT_PIECE], preferred_element_type=F32))
        yield
    ffn = jnp.concatenate(pieces, axis=1)
    for b in rows:
        x2 = x1_scr[b * CHUNK:(b + 1) * CHUNK, :] + gf_ref[b] * ffn[b * CHUNK:(b + 1) * CHUNK]
        if final:
            x2 = _rms(x2) * fin_ref[...]
        out_ref[b] = x2


def _layer_kernel(xin_ref, xres_ref, sh_ref, sc_ref, gm_ref, shf_ref, scf_ref, gf_ref, cos_ref, sin_ref,
                  dec_ref, cdec_ref, tri_ref, mask2f_ref,
                  gret_ref, mu_ref, p384_ref, w2a2_ref, g2_ref, convw_ref, convb_ref, gbias_ref, gml_ref,
                  gin_ref, win_ref, wout_ref, gff_ref, wi_ref, wo_ref, fin_ref,
                  out_ref,
                  proj_scr, mixed_scr, act_scr, x1_scr, s_ret, s_rwkv, prev_rwkv, c_ml, n_ml, m_ml, prev_qk,
                  *, n_chunks, n_tiles, final):
    s = pl.program_id(0)
    mix_tile = jnp.clip(s - 1, 0, n_tiles - 1)
    slot_in = lax.rem(s, 2)
    slot_mix = 1 - slot_in

    @pl.when(s == 0)
    def _():
        proj_scr[1] = jnp.zeros(proj_scr.shape[1:], proj_scr.dtype)
        mixed_scr[0] = jnp.zeros(mixed_scr.shape[1:], mixed_scr.dtype)

    @pl.when(lax.rem(mix_tile, n_chunks) == 0)
    def _():
        for ref in (s_ret, s_rwkv, prev_rwkv, c_ml, n_ml, m_ml, prev_qk):
            ref[...] = jnp.zeros(ref.shape, ref.dtype)

    tril, stril = tri_ref[0], tri_ref[1]
    mask2f = mask2f_ref[...]
    lo_half = lax.broadcasted_iota(jnp.int32, (1, TILE), 1) < TILE // 2
    cos = cos_ref[...]
    sin = sin_ref[...]
    proj_mix = proj_scr.at[slot_mix]
    mixed_mix = mixed_scr.at[slot_mix]
    dense = _dense_chain(xin_ref, xres_ref, sh_ref, sc_ref, gm_ref, shf_ref, scf_ref, gf_ref, gin_ref, win_ref,
                         wout_ref, gff_ref, wi_ref, wo_ref, fin_ref, out_ref, proj_scr.at[slot_in],
                         mixed_scr.at[slot_in], act_scr, x1_scr, final)
    mixers = []
    for bb in range(MIX_BB):
        r0 = bb * CHUNK
        cols = lambda lo, hi, r0=r0: proj_mix[r0:r0 + CHUNK, lo:hi]

        def writer(lo, hi, r0=r0):
            def write(y):
                mixed_mix[r0:r0 + CHUNK, lo:hi] = y.astype(mixed_mix.dtype)
            return write

        mixers.append(_rwkv_chunk(cols, writer(RET_W, RET_W + RWKV_W), tril, stril, mask2f, lo_half,
                                  mu_ref[...], p384_ref[...], w2a2_ref[...], g2_ref[...],
                                  s_rwkv.at[bb], prev_rwkv.at[bb]))
        mixers.append(_retention_chunk(cols, writer(0, RET_W), cos, sin, dec_ref, cdec_ref[...], mask2f,
                                       lo_half, gret_ref[...], s_ret.at[bb]))
        mixers.append(_mlstm_chunk(cols, writer(RET_W + RWKV_W, D_MIX_PAD), tril, stril, lo_half,
                                   convw_ref[...], convb_ref[...], gbias_ref[...], gml_ref[...],
                                   c_ml.at[bb], n_ml.at[bb], m_ml.at[bb], prev_qk.at[bb]))
    per_row = len(mixers) // MIX_BB
    first, second = mixers[:per_row * (MIX_BB // 2)], mixers[per_row * (MIX_BB // 2):]
    active = first + [dense]
    rnd = 0
    while active:
        if rnd == WAVE_DELAY:
            active = active + second
        active = [g for g in active if next(g, _DONE) is not _DONE]
        rnd += 1
    assert rnd > WAVE_DELAY


def _layer(x3, mod, consts, layer_params, final_gain, layer, final):
    batch, seq, d = x3.shape
    n_chunks = seq // CHUNK
    n_tiles = (batch // MIX_BB) * n_chunks
    rows = MIX_BB * CHUNK

    def tile(lag):
        def index(s):
            t = jnp.clip(s - lag, 0, n_tiles - 1)
            return t // n_chunks, lax.rem(t, n_chunks)
        return index

    x_spec = lambda lag: pl.BlockSpec((MIX_BB, CHUNK, d), lambda s: (*tile(lag)(s), 0))
    mod_spec = lambda lag, col: pl.BlockSpec((None, MIX_BB, 1, d), lambda s: (layer, tile(lag)(s)[0], 0, col))
    rope_spec = pl.BlockSpec((CHUNK, RET_W), lambda s: (tile(1)(s)[1], 0))
    full = lambda a: pl.BlockSpec(a.shape, lambda s: (0,) * a.ndim, pipeline_mode=pl.Buffered(1))
    of_layer = lambda a: pl.BlockSpec((None,) + a.shape[1:], lambda s: (layer,) + (0,) * (a.ndim - 1),
                                      pipeline_mode=pl.Buffered(1))
    const_inputs = [consts["dec"], consts["cdec"], consts["tri"], consts["mask2f"]]
    return pl.pallas_call(
        functools.partial(_layer_kernel, n_chunks=n_chunks, n_tiles=n_tiles, final=final),
        grid=(n_tiles + 2,),
        in_specs=[x_spec(0), x_spec(2), mod_spec(0, 0), mod_spec(0, 1), mod_spec(2, 2), mod_spec(2, 3),
                  mod_spec(2, 4), mod_spec(2, 5), rope_spec, rope_spec]
                 + [full(a) for a in const_inputs] + [of_layer(a) for a in layer_params] + [full(final_gain)],
        out_specs=x_spec(2),
        out_shape=jax.ShapeDtypeStruct((batch, seq, d), F32),
        scratch_shapes=[
            pltpu.VMEM((2, rows, D_IN_PAD), F32),
            pltpu.VMEM((2, rows, D_MIX_PAD), BF16),
            pltpu.VMEM((rows, D_FF), BF16),
            pltpu.VMEM((rows, d), F32),
            pltpu.VMEM((MIX_BB, RET_W // TILE, TILE, TILE), F32),
            pltpu.VMEM((MIX_BB, RWKV_W // TILE, TILE, TILE), F32),
            pltpu.VMEM((MIX_BB, SUB, RWKV_IN), F32),
            pltpu.VMEM((MIX_BB, MLSTM_HEADS, TILE, TILE), F32),
            pltpu.VMEM((MIX_BB, SUB, MLSTM_WP), F32),
            pltpu.VMEM((MIX_BB, SUB, TILE), F32),
            pltpu.VMEM((MIX_BB, SUB, 2 * MLSTM_WP), F32),
        ],
        compiler_params=pltpu.CompilerParams(dimension_semantics=("arbitrary",),
                                             vmem_limit_bytes=VMEM_LIMIT),
        name="layer",
    )(x3, x3, mod, mod, mod, mod, mod, mod, consts["cos"], consts["sin"],
      *const_inputs, *layer_params, final_gain)


def _pad_heads_index(n_parts):
    idx = -np.ones(n_parts * MLSTM_WP, np.int64)
    for part in range(n_parts):
        for h in range(MLSTM_HEADS):
            dst = part * MLSTM_WP + h * TILE
            idx[dst:dst + MLSTM_HD] = part * MLSTM_W + h * MLSTM_HD + np.arange(MLSTM_HD)
    return idx


def _gather_cols(a, idx):
    return jnp.where(jnp.asarray(idx >= 0), a[..., np.maximum(idx, 0)], 0.0)


def _mixer_params(mix_gn, rwkv_mu, rwkv_w0, rwkv_w2, rwkv_a0, rwkv_a2, rwkv_g2, rwkv_k_k, rwkv_k_a, rwkv_r_k,
                  mlstm_conv_w, mlstm_conv_b, mlstm_i_b, mlstm_f_b):
    depth = mix_gn.shape[0]
    zeros = jnp.zeros((depth, RWKV_W), F32)
    p384 = jnp.stack([rwkv_w0, rwkv_a0, rwkv_k_k, rwkv_k_a, rwkv_r_k.reshape(depth, RWKV_W),
                      mix_gn[:, RET_W:RET_W + RWKV_W], zeros, zeros], axis=1)
    w2a2 = jnp.zeros((depth, LORA_W, 2 * RWKV_W), F32)
    w2a2 = w2a2.at[:, :RWKV_DECAY_LORA, :RWKV_W].set(rwkv_w2).at[:, RWKV_DECAY_LORA:, RWKV_W:].set(rwkv_a2)
    gbias = jnp.zeros((depth, 1, TILE), F32)
    gbias = gbias.at[:, 0, :MLSTM_HEADS].set(mlstm_i_b).at[:, 0, MLSTM_HEADS:2 * MLSTM_HEADS].set(mlstm_f_b)
    qk_idx = _pad_heads_index(2)
    convw = jnp.zeros((depth, SUB, 2 * MLSTM_WP), F32).at[:, :MLSTM_CONV].set(_gather_cols(mlstm_conv_w, qk_idx))
    convb = _gather_cols(mlstm_conv_b, qk_idx)[:, None, :]
    gain_ml = _gather_cols(mix_gn[:, RET_W + RWKV_W:], _pad_heads_index(1))[:, None, :]
    return [mix_gn[:, None, :RET_W], rwkv_mu[:, None, :], p384, _bf(w2a2), _bf(rwkv_g2), convw, convb, gbias,
            gain_ml]


def _pad_w_in(w):
    idx = _pad_heads_index(4)
    gates = -np.ones(TILE, np.int64)
    gates[:2 * MLSTM_HEADS] = 4 * MLSTM_W + np.arange(2 * MLSTM_HEADS)
    ml = _gather_cols(w[..., MLSTM_OFF:], np.concatenate([idx, gates]))
    return jnp.concatenate([w[..., :MLSTM_OFF], ml], axis=-1)


def _pad_w_out(w):
    idx = _pad_heads_index(1)
    ml = jnp.where(jnp.asarray(idx >= 0)[:, None], w[:, RET_W + RWKV_W + np.maximum(idx, 0)], 0.0)
    return jnp.concatenate([w[:, :RET_W + RWKV_W], ml], axis=1)


def kernel(x, c, ada_w, ada_b, norm_mix, norm_ffn, w_in, mix_gn, rwkv_mu, rwkv_w0, rwkv_w2, rwkv_a0, rwkv_a2, rwkv_g2, rwkv_k_k, rwkv_k_a, rwkv_r_k, mlstm_conv_w, mlstm_conv_b, mlstm_i_b, mlstm_f_b, w_out, ffn_w_in, ffn_w_out, final_norm):
    batch, seq, d = x.shape
    depth = ada_w.shape[0]
    assert d == D_MODEL and seq % 256 == 0 and batch % MIX_BB == 0
    consts = _mixer_constants(seq)
    mod = _modulation(c, ada_w, ada_b).reshape(depth, batch, 1, 6 * d)
    mixer_params = _mixer_params(mix_gn, rwkv_mu, rwkv_w0, rwkv_w2, rwkv_a0, rwkv_a2, rwkv_g2, rwkv_k_k, rwkv_k_a,
                                 rwkv_r_k, mlstm_conv_w, mlstm_conv_b, mlstm_i_b, mlstm_f_b)
    dense_params = [norm_mix[:, None, :], _pad_w_in(_bf(w_in)), _pad_w_out(_bf(w_out)), norm_ffn[:, None, :],
                    _bf(ffn_w_in), _bf(ffn_w_out)]
    for l in range(depth):
        x = _layer(x, mod, consts, mixer_params + dense_params, final_norm[None, :], l, final=(l == depth - 1))
    return x
```

```python
import functools

import numpy as np
import jax
import jax.numpy as jnp
from jax import lax
from jax.experimental import pallas as pl
from jax.experimental.pallas import tpu as pltpu

F32 = jnp.float32
BF16 = jnp.bfloat16

D_MODEL = 1024
CHUNK = 64
RET_HEADS, RET_HD = 4, 64
RET_W = RET_HEADS * RET_HD
RWKV_HEADS, RWKV_HD = 6, 64
RWKV_W = RWKV_HEADS * RWKV_HD
MLSTM_HEADS, MLSTM_HD = 4, 96
MLSTM_W = MLSTM_HEADS * MLSTM_HD
D_MIX = RET_W + RWKV_W + MLSTM_W
RWKV_DECAY_LORA, RWKV_ICL_LORA, RWKV_GATE_LORA = 64, 64, 128
MLSTM_CONV = 4
D_FF = 2816
ROPE_BASE = 10000.0
RMS_EPS = 1e-6
GN_EPS = 1e-5

RET_IN = 4 * RET_W
RWKV_IN = 3 * RWKV_W + RWKV_DECAY_LORA + RWKV_ICL_LORA + RWKV_GATE_LORA
MLSTM_IN = 4 * MLSTM_W + 2 * MLSTM_HEADS
D_IN = RET_IN + RWKV_IN + MLSTM_IN
TILE = 128
SUB = 8
MLSTM_WP = MLSTM_HEADS * TILE
MLSTM_IN_PAD = 4 * MLSTM_WP + TILE
D_IN_PAD = RET_IN + RWKV_IN + MLSTM_IN_PAD
D_MIX_PAD = RET_W + RWKV_W + MLSTM_WP
RWKV_OFF = RET_IN
MLSTM_OFF = RET_IN + RWKV_IN
LORA_W = RWKV_DECAY_LORA + RWKV_ICL_LORA
FF_CHUNK = 256
IN_PIECE = 512
OUT_PIECE = 256
MIX_BB = 4

WAVE_DELAY = 7
VMEM_LIMIT = 60000 * 1024
_DONE = object()


def _bf(x):
    return x.astype(BF16)


def _mm(a, b):
    return jnp.dot(_bf(a), _bf(b), preferred_element_type=F32)


def _mm_nt(a, b):
    return lax.dot_general(_bf(a), _bf(b), (((1,), (1,)), ((), ())), preferred_element_type=F32)


def _mm_tn(a, b):
    return lax.dot_general(_bf(a), _bf(b), (((0,), (0,)), ((), ())), preferred_element_type=F32)


def _tiles(x):
    return [x[:, t * TILE:(t + 1) * TILE] for t in range(x.shape[1] // TILE)]


def _cumsum_rows(x):
    row = lax.broadcasted_iota(jnp.int32, (x.shape[0], 1), 0)
    k = 1
    while k < x.shape[0]:
        x = x + jnp.where(row >= k, pltpu.roll(x, k, 0), 0.0)
        k *= 2
    return x


def _segsum64(x, lo_half):
    outs = []
    for t in _tiles(x):
        lo = jnp.sum(jnp.where(lo_half, t, 0.0), axis=1, keepdims=True)
        hi = jnp.sum(jnp.where(lo_half, 0.0, t), axis=1, keepdims=True)
        outs.append(jnp.where(lo_half, lo, hi))
    return jnp.concatenate(outs, axis=1)


def _bd2(y, lo_half):
    yb = _bf(y)
    zero = jnp.zeros_like(yb)
    return jnp.concatenate([jnp.where(lo_half, yb, zero), jnp.where(lo_half, zero, yb)], axis=0)


def _log_sigmoid(x):
    return jnp.minimum(x, 0.0) - jnp.log(1.0 + jnp.exp(-jnp.abs(x)))


def _silu(x):
    return x * jax.nn.sigmoid(x)


def _rms(x):
    return x * lax.rsqrt(jnp.mean(x * x, axis=-1, keepdims=True) + RMS_EPS)


def _shift_rows(prev8, cur, k):
    rolled = pltpu.roll(cur, k, 0)
    row8 = lax.broadcasted_iota(jnp.int32, (SUB, 1), 0)
    first = jnp.where(row8 < k, pltpu.roll(prev8, k, 0), rolled[0:SUB])
    return jnp.concatenate([first, rolled[SUB:]], axis=0)


def _mod_kernel(c_ref, w_ref, b_ref, o_ref):
    cond = _silu(c_ref[...])
    o_ref[0] = _mm(cond, w_ref[0]) + b_ref[0]


def _modulation(c, ada_w, ada_b):
    depth, d, n = ada_w.shape
    b = c.shape[0]
    tn = 1536
    return pl.pallas_call(
        _mod_kernel,
        grid=(depth, n // tn),
        in_specs=[
            pl.BlockSpec((b, d), lambda l, j: (0, 0)),
            pl.BlockSpec((1, d, tn), lambda l, j: (l, 0, j)),
            pl.BlockSpec((1, 1, tn), lambda l, j: (l, 0, j)),
        ],
        out_specs=pl.BlockSpec((1, b, tn), lambda l, j: (l, 0, j)),
        out_shape=jax.ShapeDtypeStruct((depth, b, n), F32),
        compiler_params=pltpu.CompilerParams(dimension_semantics=("arbitrary", "arbitrary"),
                                             vmem_limit_bytes=VMEM_LIMIT),
        name="adaln_modulation",
    )(c, ada_w, ada_b.reshape(depth, 1, n))


def _retention_chunk(cols, write, cos, sin, dec_ref, cdec, mask2f, lo_half, gain, state_ref):
    w = RET_W
    lane = lax.broadcasted_iota(jnp.int32, (CHUNK, w), 1)
    first_half = (lane & (RET_HD - 1)) < RET_HD // 2

    def rope(z):
        swapped = jnp.where(first_half, pltpu.roll(z, w - RET_HD // 2, 1), pltpu.roll(z, RET_HD // 2, 1))
        return z * cos + swapped * sin

    q = rope(cols(0, w))
    k = rope(cols(w, 2 * w)) * (RET_HD ** -0.5)
    v = cols(2 * w, 3 * w)
    g = cols(3 * w, 4 * w)
    intra, q_dec = _tiles(dec_ref[0]), _tiles(dec_ref[1])
    q_t, k_t, v_t, kd_t = _tiles(q), _tiles(k), _tiles(v), _tiles(k * dec_ref[2])
    nt = len(q_t)
    states = [state_ref[t] for t in range(nt)]
    scores = [_mm_nt(q_t[t], _bd2(k_t[t], lo_half)) * intra[t] for t in range(nt)]
    inter = [_mm(q_t[t], states[t]) * q_dec[t] for t in range(nt)]
    yield
    y = jnp.concatenate([_mm(scores[t], _bd2(v_t[t], lo_half)) + inter[t] for t in range(nt)], axis=1)
    for t in range(nt):
        state_ref[t] = states[t] * cdec[:, t * TILE:(t + 1) * TILE] + mask2f * _mm_tn(kd_t[t], v_t[t])
    yield
    yc = y - _segsum64(y, lo_half) * (1.0 / RET_HD)
    var = _segsum64(yc * yc, lo_half) * (1.0 / RET_HD)
    write(yc * lax.rsqrt(var + GN_EPS) * gain * _silu(g))


def _rwkv_chunk(cols, write, tril, stril, mask2f, lo_half, mu, p384, w2a2, g2, state_ref, prev_ref):
    w = RWKV_W
    x = cols(RWKV_OFF, RWKV_OFF + RWKV_IN)
    prev = _shift_rows(prev_ref[...], x, 1)
    prev_ref[...] = x[CHUNK - SUB:CHUNK]
    xs = x + (prev - x) * mu
    r = xs[:, 0:w]
    k = xs[:, w:2 * w]
    v = xs[:, 2 * w:3 * w]
    lora = xs[:, 3 * w:3 * w + LORA_W]
    gl = xs[:, 3 * w + LORA_W:RWKV_IN]
    w0, a0, k_k, k_a, r_k, gain = (p384[i:i + 1, :] for i in range(6))

    lane = lax.broadcasted_iota(jnp.int32, (CHUNK, LORA_W), 1)
    wa = _mm(jnp.where(lane < RWKV_DECAY_LORA, jnp.tanh(lora), lora), w2a2)
    g = _mm(jax.nn.sigmoid(gl), g2)
    kk = k * k_k
    kk_sq = _segsum64(kk * kk, lo_half)
    yield
    log_decay = -jnp.exp(_log_sigmoid(w0 + wa[:, 0:w]) - 0.5)
    a = jax.nn.sigmoid(a0 + wa[:, w:2 * w])
    kk = kk / jnp.maximum(jnp.sqrt(kk_sq), 1e-12)
    k = k * (1.0 + (a - 1.0) * k_a)
    bonus = _segsum64(r * k * r_k, lo_half) * v
    c = _cumsum_rows(log_decay)
    yield
    c_last = c[CHUNK - 1:CHUNK, :]
    p_inv = jnp.exp(-c)
    p_end = jnp.exp(c_last - c)
    p_last = jnp.exp(c_last)
    rh = r * jnp.exp(c)
    bh = -kk * jnp.exp(c - log_decay)
    ka = kk * a
    lhs_t = _tiles(jnp.concatenate([bh, rh], axis=0))
    upd_t = _tiles(jnp.concatenate([ka * p_end, k * p_end], axis=0))
    ah_t, kh_t, v_t = _tiles(ka * p_inv), _tiles(k * p_inv), _tiles(v)
    nt = len(v_t)
    tiles = range(nt)

    states = [state_ref[t] for t in tiles]
    s_all = [_mm_nt(lhs_t[t], jnp.concatenate([_bd2(ah_t[t], lo_half), _bd2(kh_t[t], lo_half), _bf(states[t])],
                                              axis=0)) for t in tiles]
    s_a = [z[:, 0:TILE] for z in s_all]
    from_state = [z[:, 2 * TILE:3 * TILE] for z in s_all]
    bd_v = [_bd2(v_t[t], lo_half) for t in tiles]
    tri2 = jnp.concatenate([stril, tril], axis=0)
    yield
    uy = [from_state[t] + _mm(s_all[t][:, TILE:2 * TILE] * tri2, bd_v[t]) for t in tiles]
    u = [z[0:CHUNK] for z in uy]
    y_part = [z[CHUNK:] for z in uy]
    n_pow = [s_a[t][0:CHUNK] * stril for t in tiles]
    yield
    steps = CHUNK.bit_length() - 1
    for j in range(steps):
        if j + 1 < steps:
            both = [_mm(n_pow[t], jnp.concatenate([_bd2(u[t], lo_half), _bd2(n_pow[t], lo_half)], axis=1))
                    for t in tiles]
            u = [u[t] + both[t][:, 0:TILE] for t in tiles]
            n_pow = [both[t][:, TILE:2 * TILE] for t in tiles]
        else:
            u = [u[t] + _mm(n_pow[t], _bd2(u[t], lo_half)) for t in tiles]
        yield
    y = jnp.concatenate([y_part[t] + _mm(s_a[t][CHUNK:] * tril, _bd2(u[t], lo_half)) for t in tiles], axis=1)
    for t in tiles:
        upd = _mm_tn(jnp.concatenate([u[t], v_t[t]], axis=0), upd_t[t])
        state_ref[t] = states[t] * p_last[:, t * TILE:(t + 1) * TILE] + mask2f * upd
    yield
    yc = y - _segsum64(y, lo_half) * (1.0 / RWKV_HD)
    var = _segsum64(yc * yc, lo_half) * (1.0 / RWKV_HD)
    write((yc * lax.rsqrt(var + GN_EPS) * gain + bonus) * g)


def _mlstm_chunk(cols, write, tril, stril, lo_half, convw, convb, gbias, gain, c_ref, n_ref, m_ref, prev_ref):
    wp = MLSTM_WP
    nh = MLSTM_HEADS
    lane_g = lax.broadcasted_iota(jnp.int32, (1, TILE), 1)

    def col(x, h):
        return jnp.sum(jnp.where(lane_g == h, x, 0.0), axis=1, keepdims=True)

    def spread_s(cs):
        return jnp.concatenate([jnp.where(lo_half, cs[2 * p], cs[2 * p + 1]) for p in range(nh // 2)], axis=1)

    def spread_w(cs):
        return jnp.concatenate([jnp.broadcast_to(c, (c.shape[0], TILE)) for c in cs], axis=1)

    off = MLSTM_OFF
    qk_pre = cols(off, off + 2 * wp)
    prev8 = prev_ref[...]
    prev_ref[...] = qk_pre[CHUNK - SUB:CHUNK]
    conv = convb + qk_pre * convw[MLSTM_CONV - 1:MLSTM_CONV, :]
    for j in range(MLSTM_CONV - 1):
        conv = conv + _shift_rows(prev8, qk_pre, MLSTM_CONV - 1 - j) * convw[j:j + 1, :]
    qk = _silu(conv)
    q = qk[:, 0:wp]
    k = qk[:, wp:2 * wp] * (MLSTM_HD ** -0.5)
    v = cols(off + 2 * wp, off + 3 * wp)
    o = cols(off + 3 * wp, off + 4 * wp)
    gates = cols(off + 4 * wp, off + 4 * wp + TILE) + gbias
    i_pre = gates
    log_f = _log_sigmoid(pltpu.roll(gates, TILE - nh, 1))
    bcum = _cumsum_rows(log_f)
    yield
    m_prev = m_ref[0:1, :]

    tril2 = jnp.concatenate([tril] * (nh // 2), axis=1)
    eye2 = tril2 - jnp.concatenate([stril] * (nh // 2), axis=1)
    bcum_c = [col(bcum, h) for h in range(nh)]
    src_c = [col(i_pre - bcum, h) for h in range(nh)]
    src_row = jnp.sum(spread_s(src_c) * eye2, axis=0, keepdims=True)
    d_mat = jnp.where(tril2 > 0.0, spread_s(bcum_c) + src_row, -jnp.inf)
    a_c = [bcum_c[h] + col(m_prev, h) for h in range(nh)]
    head_lanes = [lo_half, jnp.logical_not(lo_half)]
    d_tiles = _tiles(d_mat)
    m_c = [jnp.maximum(a_c[h], jnp.max(jnp.where(head_lanes[h % 2], d_tiles[h // 2], -jnp.inf),
                                       axis=1, keepdims=True)) for h in range(nh)]
    inter_c = [jnp.exp(a_c[h] - m_c[h]) for h in range(nh)]
    weights = jnp.exp(d_mat - spread_s(m_c))

    q_t, k_t, v_t = _tiles(q), _tiles(k), _tiles(v)
    zero = jnp.zeros((CHUNK, TILE), BF16)

    def pair_bd(z_t, p):
        a, b = _bf(z_t[2 * p]), _bf(z_t[2 * p + 1])
        return jnp.concatenate([jnp.concatenate([a, zero], axis=1), jnp.concatenate([zero, b], axis=1)], axis=0)

    pairs = range(nh // 2)
    heads = range(nh)
    c_states = [c_ref[h] for h in heads]
    raw = [_mm_nt(jnp.concatenate([q_t[2 * p], q_t[2 * p + 1]], axis=1), pair_bd(k_t, p)) for p in pairs]
    from_state = [_mm(q_t[h], c_states[h]) for h in heads]
    yield
    s_tiles = [raw[p] * weights[:, p * TILE:(p + 1) * TILE] for p in pairs]
    num_tiles = [z for p in pairs for z in _tiles(_mm(s_tiles[p], pair_bd(v_t, p)))]
    n_state = n_ref[0:1, :]
    qn_t = _tiles(q * n_state)
    b_last = bcum[CHUNK - 1:CHUNK, :]
    g_s = b_last - bcum + i_pre
    m_new = jnp.maximum(b_last + m_prev, jnp.max(g_s, axis=0, keepdims=True))
    ws = jnp.exp(g_s - m_new)
    carry = jnp.exp(b_last + m_prev - m_new)
    kw_t = [k_t[h] * col(ws, h) for h in heads]
    for h in heads:
        c_ref[h] = c_states[h] * col(carry, h) + _mm_tn(kw_t[h], v_t[h])
    n_ref[0:1, :] = (n_state * spread_w([col(carry, h) for h in heads])
                     + jnp.sum(jnp.concatenate(kw_t, axis=1), axis=0, keepdims=True))
    m_ref[0:1, :] = m_new
    yield
    pad_mask = jnp.where(lane_g < MLSTM_HD, 1.0, 0.0)
    hs = []
    for h in heads:
        num = num_tiles[h] + inter_c[h] * from_state[h]
        den = (jnp.sum(jnp.where(head_lanes[h % 2], s_tiles[h // 2], 0.0), axis=1, keepdims=True)
               + inter_c[h] * jnp.sum(qn_t[h], axis=1, keepdims=True))
        hc = num / jnp.maximum(jnp.abs(den), jnp.exp(-m_c[h]))
        mean = jnp.sum(hc, axis=1, keepdims=True) * (1.0 / MLSTM_HD)
        yc = (hc - mean) * pad_mask
        var = jnp.sum(yc * yc, axis=1, keepdims=True) * (1.0 / MLSTM_HD)
        hs.append(yc * lax.rsqrt(var + GN_EPS))
    write(jnp.concatenate(hs, axis=1) * gain * jax.nn.sigmoid(o))


def _mixer_constants(seq):
    h = jnp.arange(RET_HEADS, dtype=F32)
    log_g = jnp.log1p(-jnp.exp2(-5.0 - h))
    idx = jnp.arange(CHUNK, dtype=F32)
    diff = idx[:, None] - idx[None, :]
    intra = jnp.where(diff >= 0, jnp.exp(log_g[:, None, None] * jnp.maximum(diff, 0.0)), 0.0)
    intra = jnp.transpose(intra, (1, 0, 2)).reshape(CHUNK, RET_HEADS * CHUNK)
    q_decay = jnp.exp(log_g[:, None] * (idx + 1.0))
    k_decay = jnp.exp(log_g[:, None] * (CHUNK - 1.0 - idx))
    rep = lambda z: jnp.repeat(z.T, RET_HD, axis=1)
    dec = jnp.stack([intra, rep(q_decay), rep(k_decay)])
    cdec = jnp.repeat(jnp.exp(log_g * CHUNK), RET_HD)[None, :]

    inv = ROPE_BASE ** (-jnp.arange(0, RET_HD, 2, dtype=F32) / RET_HD)
    ang = jnp.arange(seq, dtype=F32)[:, None] * inv[None, :]
    cos = jnp.tile(jnp.cos(ang), (1, 2 * RET_HEADS))
    sin = jnp.tile(jnp.concatenate([-jnp.sin(ang), jnp.sin(ang)], axis=1), (1, RET_HEADS))

    half = np.arange(TILE) // CHUNK
    mask2 = (half[:, None] == half[None, :]).astype(np.float32)
    t = np.arange(CHUNK)
    tril = (t[:, None] >= t[None, :]).astype(np.float32)
    stril = (t[:, None] > t[None, :]).astype(np.float32)
    tri = np.stack([np.tile(tril, (1, 2)), np.tile(stril, (1, 2))])
    return dict(cos=cos, sin=sin, dec=dec, cdec=cdec, tri=jnp.asarray(tri), mask2f=jnp.asarray(mask2))


def _dense_chain(xin_ref, xres_ref, sh_ref, sc_ref, gm_ref, shf_ref, scf_ref, gf_ref, gin_ref, win_ref,
                 wout_ref, gff_ref, wi_ref, wo_ref, fin_ref, out_ref, proj_slot, mixed_slot, act_scr, x1_scr,
                 final):
    rows = range(MIX_BB)
    h = jnp.concatenate([_bf(_rms(xin_ref[b]) * gin_ref[...] * (1.0 + sc_ref[b]) + sh_ref[b]) for b in rows],
                        axis=0)
    for n0 in range(0, D_IN_PAD, IN_PIECE):
        proj_slot[:, n0:n0 + IN_PIECE] = _bf(jnp.dot(h, win_ref[:, n0:n0 + IN_PIECE], preferred_element_type=F32))
        yield
    att = jnp.dot(mixed_slot[...], wout_ref[...], preferred_element_type=F32)
    yield
    hs = []
    for b in rows:
        x1 = xres_ref[b] + gm_ref[b] * att[b * CHUNK:(b + 1) * CHUNK]
        x1_scr[b * CHUNK:(b + 1) * CHUNK, :] = x1
        hs.append(_bf(_rms(x1) * gff_ref[...] * (1.0 + scf_ref[b]) + shf_ref[b]))
    h = jnp.concatenate(hs, axis=0)
    for f0 in range(0, D_FF, FF_CHUNK):
        gate = jnp.dot(h, wi_ref[:, f0:f0 + FF_CHUNK], preferred_element_type=F32)
        up = jnp.dot(h, wi_ref[:, D_FF + f0:D_FF + f0 + FF_CHUNK], preferred_element_type=F32)
        act_scr[:, f0:f0 + FF_CHUNK] = _bf(_silu(gate) * up)
        yield
    pieces = []
    for n0 in range(0, D_MODEL, OUT_PIECE):
        pieces.append(jnp.dot(act_scr[...], wo_ref[:, n0:n0 + OUT_PIECE], preferred_element_type=F32))
        yield
    ffn = jnp.concatenate(pieces, axis=1)
    for b in rows:
        x2 = x1_scr[b * CHUNK:(b + 1) * CHUNK, :] + gf_ref[b] * ffn[b * CHUNK:(b + 1) * CHUNK]
        if final:
            x2 = _rms(x2) * fin_ref[...]
        out_ref[b] = x2


def _layer_kernel(xin_ref, xres_ref, sh_ref, sc_ref, gm_ref, shf_ref, scf_ref, gf_ref, cos_ref, sin_ref,
                  dec_ref, cdec_ref, tri_ref, mask2f_ref,
                  gret_ref, mu_ref, p384_ref, w2a2_ref, g2_ref, convw_ref, convb_ref, gbias_ref, gml_ref,
                  gin_ref, win_ref, wout_ref, gff_ref, wi_ref, wo_ref, fin_ref,
                  out_ref,
                  proj_scr, mixed_scr, act_scr, x1_scr, s_ret, s_rwkv, prev_rwkv, c_ml, n_ml, m_ml, prev_qk,
                  *, n_chunks, n_tiles, final):
    s = pl.program_id(0)
    mix_tile = jnp.clip(s - 1, 0, n_tiles - 1)
    slot_in = lax.rem(s, 2)
    slot_mix = 1 - slot_in

    @pl.when(s == 0)
    def _():
        proj_scr[1] = jnp.zeros(proj_scr.shape[1:], proj_scr.dtype)
        mixed_scr[0] = jnp.zeros(mixed_scr.shape[1:], mixed_scr.dtype)

    @pl.when(lax.rem(mix_tile, n_chunks) == 0)
    def _():
        for ref in (s_ret, s_rwkv, prev_rwkv, c_ml, n_ml, m_ml, prev_qk):
            ref[...] = jnp.zeros(ref.shape, ref.dtype)

    tril, stril = tri_ref[0], tri_ref[1]
    mask2f = mask2f_ref[...]
    lo_half = lax.broadcasted_iota(jnp.int32, (1, TILE), 1) < TILE // 2
    cos = cos_ref[...]
    sin = sin_ref[...]
    proj_mix = proj_scr.at[slot_mix]
    mixed_mix = mixed_scr.at[slot_mix]
    dense = _dense_chain(xin_ref, xres_ref, sh_ref, sc_ref, gm_ref, shf_ref, scf_ref, gf_ref, gin_ref, win_ref,
                         wout_ref, gff_ref, wi_ref, wo_ref, fin_ref, out_ref, proj_scr.at[slot_in],
                         mixed_scr.at[slot_in], act_scr, x1_scr, final)
    mixers = []
    for bb in range(MIX_BB):
        r0 = bb * CHUNK
        cols = lambda lo, hi, r0=r0: proj_mix[r0:r0 + CHUNK, lo:hi].astype(F32)

        def writer(lo, hi, r0=r0):
            def write(y):
                mixed_mix[r0:r0 + CHUNK, lo:hi] = y.astype(mixed_mix.dtype)
            return write

        mixers.append(_rwkv_chunk(cols, writer(RET_W, RET_W + RWKV_W), tril, stril, mask2f, lo_half,
                                  mu_ref[...], p384_ref[...], w2a2_ref[...], g2_ref[...],
                                  s_rwkv.at[bb], prev_rwkv.at[bb]))
        mixers.append(_retention_chunk(cols, writer(0, RET_W), cos, sin, dec_ref, cdec_ref[...], mask2f,
                                       lo_half, gret_ref[...], s_ret.at[bb]))
        mixers.append(_mlstm_chunk(cols, writer(RET_W + RWKV_W, D_MIX_PAD), tril, stril, lo_half,
                                   convw_ref[...], convb_ref[...], gbias_ref[...], gml_ref[...],
                                   c_ml.at[bb], n_ml.at[bb], m_ml.at[bb], prev_qk.at[bb]))
    per_row = len(mixers) // MIX_BB
    first, second = mixers[:per_row * (MIX_BB // 2)], mixers[per_row * (MIX_BB // 2):]
    active = first + [dense]
    rnd = 0
    while active:
        if rnd == WAVE_DELAY:
            active = active + second
        active = [g for g in active if next(g, _DONE) is not _DONE]
        rnd += 1
    assert rnd > WAVE_DELAY


def _layer(x3, mod, consts, layer_params, final_gain, layer, final):
    batch, seq, d = x3.shape
    n_chunks = seq // CHUNK
    n_tiles = (batch // MIX_BB) * n_chunks
    rows = MIX_BB * CHUNK

    def tile(lag):
        def index(s):
            t = jnp.clip(s - lag, 0, n_tiles - 1)
            return t // n_chunks, lax.rem(t, n_chunks)
        return index

    x_spec = lambda lag: pl.BlockSpec((MIX_BB, CHUNK, d), lambda s: (*tile(lag)(s), 0))
    mod_spec = lambda lag, col: pl.BlockSpec((None, MIX_BB, 1, d), lambda s: (layer, tile(lag)(s)[0], 0, col))
    rope_spec = pl.BlockSpec((CHUNK, RET_W), lambda s: (tile(1)(s)[1], 0))
    full = lambda a: pl.BlockSpec(a.shape, lambda s: (0,) * a.ndim, pipeline_mode=pl.Buffered(1))
    of_layer = lambda a: pl.BlockSpec((None,) + a.shape[1:], lambda s: (layer,) + (0,) * (a.ndim - 1),
                                      pipeline_mode=pl.Buffered(1))
    const_inputs = [consts["dec"], consts["cdec"], consts["tri"], consts["mask2f"]]
    return pl.pallas_call(
        functools.partial(_layer_kernel, n_chunks=n_chunks, n_tiles=n_tiles, final=final),
        grid=(n_tiles + 2,),
        in_specs=[x_spec(0), x_spec(2), mod_spec(0, 0), mod_spec(0, 1), mod_spec(2, 2), mod_spec(2, 3),
                  mod_spec(2, 4), mod_spec(2, 5), rope_spec, rope_spec]
                 + [full(a) for a in const_inputs] + [of_layer(a) for a in layer_params] + [full(final_gain)],
        out_specs=x_spec(2),
        out_shape=jax.ShapeDtypeStruct((batch, seq, d), F32),
        scratch_shapes=[
            pltpu.VMEM((2, rows, D_IN_PAD), BF16),
            pltpu.VMEM((2, rows, D_MIX_PAD), BF16),
            pltpu.VMEM((rows, D_FF), BF16),
            pltpu.VMEM((rows, d), F32),
            pltpu.VMEM((MIX_BB, RET_W // TILE, TILE, TILE), F32),
            pltpu.VMEM((MIX_BB, RWKV_W // TILE, TILE, TILE), F32),
            pltpu.VMEM((MIX_BB, SUB, RWKV_IN), F32),
            pltpu.VMEM((MIX_BB, MLSTM_HEADS, TILE, TILE), F32),
            pltpu.VMEM((MIX_BB, SUB, MLSTM_WP), F32),
            pltpu.VMEM((MIX_BB, SUB, TILE), F32),
            pltpu.VMEM((MIX_BB, SUB, 2 * MLSTM_WP), F32),
        ],
        compiler_params=pltpu.CompilerParams(dimension_semantics=("arbitrary",),
                                             vmem_limit_bytes=VMEM_LIMIT),
        name="layer",
    )(x3, x3, mod, mod, mod, mod, mod, mod, consts["cos"], consts["sin"],
      *const_inputs, *layer_params, final_gain)


def _pad_heads_index(n_parts):
    idx = -np.ones(n_parts * MLSTM_WP, np.int64)
    for part in range(n_parts):
        for h in range(MLSTM_HEADS):
            dst = part * MLSTM_WP + h * TILE
            idx[dst:dst + MLSTM_HD] = part * MLSTM_W + h * MLSTM_HD + np.arange(MLSTM_HD)
    return idx


def _gather_cols(a, idx):
    return jnp.where(jnp.asarray(idx >= 0), a[..., np.maximum(idx, 0)], 0.0)


def _mixer_params(mix_gn, rwkv_mu, rwkv_w0, rwkv_w2, rwkv_a0, rwkv_a2, rwkv_g2, rwkv_k_k, rwkv_k_a, rwkv_r_k,
                  mlstm_conv_w, mlstm_conv_b, mlstm_i_b, mlstm_f_b):
    depth = mix_gn.shape[0]
    zeros = jnp.zeros((depth, RWKV_W), F32)
    p384 = jnp.stack([rwkv_w0, rwkv_a0, rwkv_k_k, rwkv_k_a, rwkv_r_k.reshape(depth, RWKV_W),
                      mix_gn[:, RET_W:RET_W + RWKV_W], zeros, zeros], axis=1)
    w2a2 = jnp.zeros((depth, LORA_W, 2 * RWKV_W), F32)
    w2a2 = w2a2.at[:, :RWKV_DECAY_LORA, :RWKV_W].set(rwkv_w2).at[:, RWKV_DECAY_LORA:, RWKV_W:].set(rwkv_a2)
    gbias = jnp.zeros((depth, 1, TILE), F32)
    gbias = gbias.at[:, 0, :MLSTM_HEADS].set(mlstm_i_b).at[:, 0, MLSTM_HEADS:2 * MLSTM_HEADS].set(mlstm_f_b)
    qk_idx = _pad_heads_index(2)
    convw = jnp.zeros((depth, SUB, 2 * MLSTM_WP), F32).at[:, :MLSTM_CONV].set(_gather_cols(mlstm_conv_w, qk_idx))
    convb = _gather_cols(mlstm_conv_b, qk_idx)[:, None, :]
    gain_ml = _gather_cols(mix_gn[:, RET_W + RWKV_W:], _pad_heads_index(1))[:, None, :]
    return [mix_gn[:, None, :RET_W], rwkv_mu[:, None, :], p384, _bf(w2a2), _bf(rwkv_g2), convw, convb, gbias,
            gain_ml]


def _pad_w_in(w):
    idx = _pad_heads_index(4)
    gates = -np.ones(TILE, np.int64)
    gates[:2 * MLSTM_HEADS] = 4 * MLSTM_W + np.arange(2 * MLSTM_HEADS)
    ml = _gather_cols(w[..., MLSTM_OFF:], np.concatenate([idx, gates]))
    return jnp.concatenate([w[..., :MLSTM_OFF], ml], axis=-1)


def _pad_w_out(w):
    idx = _pad_heads_index(1)
    ml = jnp.where(jnp.asarray(idx >= 0)[:, None], w[:, RET_W + RWKV_W + np.maximum(idx, 0)], 0.0)
    return jnp.concatenate([w[:, :RET_W + RWKV_W], ml], axis=1)


def kernel(x, c, ada_w, ada_b, norm_mix, norm_ffn, w_in, mix_gn, rwkv_mu, rwkv_w0, rwkv_w2, rwkv_a0, rwkv_a2, rwkv_g2, rwkv_k_k, rwkv_k_a, rwkv_r_k, mlstm_conv_w, mlstm_conv_b, mlstm_i_b, mlstm_f_b, w_out, ffn_w_in, ffn_w_out, final_norm):
    batch, seq, d = x.shape
    depth = ada_w.shape[0]
    assert d == D_MODEL and seq % 256 == 0 and batch % MIX_BB == 0
    consts = _mixer_constants(seq)
    mod = _modulation(c, ada_w, ada_b).reshape(depth, batch, 1, 6 * d)
    mixer_params = _mixer_params(mix_gn, rwkv_mu, rwkv_w0, rwkv_w2, rwkv_a0, rwkv_a2, rwkv_g2, rwkv_k_k, rwkv_k_a,
                                 rwkv_r_k, mlstm_conv_w, mlstm_conv_b, mlstm_i_b, mlstm_f_b)
    dense_params = [norm_mix[:, None, :], _pad_w_in(_bf(w_in)), _pad_w_out(_bf(w_out)), norm_ffn[:, None, :],
                    _bf(ffn_w_in), _bf(ffn_w_out)]
    for l in range(depth):
        x = _layer(x, mod, consts, mixer_params + dense_params, final_norm[None, :], l, final=(l == depth - 1))
    return x
```
